```python
import math
import jax, jax.numpy as jnp
from jax import lax
import numpy as np

D_MODEL = 4096
BATCH = 4
SEQ = 2048
DEPTH = 4
DEC_BATCH = 8
DEC_SEQ = 1
PAST_LEN = 8192
PAGE_SIZE = 128

DH_A = 128
H_A = (3 * D_MODEL) // (8 * DH_A)
W_A = H_A * DH_A
MOBA_BLOCK = 256
MOBA_TOPK = 3
MOBA_QCHUNK = 16
DK_B = 64
DV_B = 2 * DK_B
H_B = D_MODEL // (4 * DV_B)
W_B = H_B * DV_B
DIFF_QBLOCK = 128
CG_C = 128
G_C = (3 * D_MODEL) // (8 * CG_C)
W_C = G_C * CG_C
CHUNK_C = 128
MIX_WIDTH = W_A + W_B + W_C
IN_SIZES = (W_A, W_A, W_A, W_B, W_B, W_B, W_C, W_C)
IN_WIDTH = 3 * W_A + 3 * W_B + 2 * W_C
ROPE_THETA = 500000.0
PEER_HEADS = 8
N_KEYS = 128
N_EXPERTS = N_KEYS * N_KEYS
D_KEY = 256
PEER_TOPK = 16
PEER_TCHUNK = 128
ALPHA = (2 * DEPTH) ** 0.25
BETA = (8 * DEPTH) ** -0.25
LN_EPS = 1e-5

kernel_name = 'hybrid_moba_diff_gmlp_peer_decoder_step'


def layer_norm(x, g, b):
    xf = x.astype(jnp.float32)
    mu = jnp.mean(xf, -1, keepdims=True)
    var = jnp.mean(jnp.square(xf - mu), -1, keepdims=True)
    return ((xf - mu) * lax.rsqrt(var + LN_EPS)).astype(x.dtype) * g + b


def rms_norm(x, g):
    xf = x.astype(jnp.float32)
    return (xf * lax.rsqrt(jnp.mean(xf * xf, -1, keepdims=True) + LN_EPS)).astype(x.dtype) * g


def partial_rope(x, pos):
    r = x.shape[-1] // 4
    half = r // 2
    inv = ROPE_THETA ** (-jnp.arange(half, dtype=jnp.float32) * (2.0 / r))
    ang = pos.astype(jnp.float32)[:, None] * inv[None, :]
    cos = jnp.cos(ang)[:, None, :]
    sin = jnp.sin(ang)[:, None, :]
    x1 = x[..., :half].astype(jnp.float32)
    x2 = x[..., half:r].astype(jnp.float32)
    rot = jnp.concatenate([x1 * cos - x2 * sin, x2 * cos + x1 * sin], -1).astype(x.dtype)
    return jnp.concatenate([rot, x[..., r:]], -1)


def gather_pages(pool, page_table):
    g = pool[page_table]
    return g.reshape((page_table.shape[0], page_table.shape[1] * pool.shape[1]) + pool.shape[2:])


def moba_attention(q, k, v, q_start, q_chunk):
    B, Q, H, Dh = q.shape
    L = k.shape[1]
    nb = -(-L // MOBA_BLOCK)
    pad = nb * MOBA_BLOCK - L
    kb = jnp.pad(k, ((0, 0), (0, pad), (0, 0), (0, 0))).reshape(B, nb, MOBA_BLOCK, H, Dh).transpose(0, 3, 1, 2, 4)
    vb = jnp.pad(v, ((0, 0), (0, pad), (0, 0), (0, 0))).reshape(B, nb, MOBA_BLOCK, H, Dh).transpose(0, 3, 1, 2, 4)
    kmean = jnp.mean(kb.astype(jnp.float32), axis=3)
    n_sel = min(MOBA_TOPK, nb)
    n_chunks = Q // q_chunk
    qc = q.reshape(B, n_chunks, q_chunk, H, Dh).transpose(1, 0, 3, 2, 4)
    key_off = jnp.arange(MOBA_BLOCK)
    bi = jnp.arange(B)[:, None, None, None]
    hi = jnp.arange(H)[None, :, None, None]
    scale = Dh ** -0.5

    def one_chunk(args):
        qi, c = args
        pos = q_start + c * q_chunk + jnp.arange(q_chunk)
        own = pos // MOBA_BLOCK
        gate = jnp.einsum('bhqd,bhnd->bhqn', qi.astype(jnp.float32), kmean)
        fully_past = jnp.arange(nb)[None, :] < own[:, None]
        gate = jnp.where(fully_past, gate, -jnp.inf)
        _, sel = lax.top_k(gate, n_sel)
        own_b = jnp.broadcast_to(own[:, None], (B, H, q_chunk, 1))
        idx = jnp.concatenate([sel, own_b], -1)
        slot_ok = jnp.concatenate([sel < own[:, None], jnp.ones((B, H, q_chunk, 1), bool)], -1)
        kg = kb[bi, hi, idx]
        vg = vb[bi, hi, idx]
        s = jnp.einsum('bhqd,bhqnkd->bhqnk', qi, kg).astype(jnp.float32) * scale
        kpos = idx[..., None] * MOBA_BLOCK + key_off
        ok = slot_ok[..., None] & (kpos <= pos[:, None, None])
        s = jnp.where(ok, s, -jnp.inf)
        p = jax.nn.softmax(s.reshape(B, H, q_chunk, -1), axis=-1).reshape(s.shape)
        return jnp.einsum('bhqnk,bhqnkd->bhqd', p.astype(v.dtype), vg)

    o = lax.map(one_chunk, (qc, jnp.arange(n_chunks)))
    return o.transpose(1, 0, 3, 2, 4).reshape(B, Q, H, Dh)


def diff_attention(q, k, v, q_start, q_block, lam, gain, lam_init):
    B, Q, H, _, dk = q.shape
    L = k.shape[1]
    nq = Q // q_block
    qb = q.reshape(B, nq, q_block, H, 2, dk).transpose(1, 0, 2, 3, 4, 5)
    kpos = jnp.arange(L)
    scale = dk ** -0.5

    def one_block(args):
        qi, c = args
        pos = q_start + c * q_block + jnp.arange(q_block)
        s = jnp.einsum('bqhmd,bkhmd->bhmqk', qi, k).astype(jnp.float32) * scale
        s = jnp.where(kpos[None, :] <= pos[:, None], s, -jnp.inf)
        p = jax.nn.softmax(s, axis=-1)
        a = p[:, :, 0] - lam * p[:, :, 1]
        return jnp.einsum('bhqk,bkhd->bqhd', a.astype(v.dtype), v)

    o = lax.map(one_block, (qb, jnp.arange(nq)))
    o = o.transpose(1, 0, 2, 3, 4).reshape(B, Q, H, v.shape[-1])
    return rms_norm(o, gain) * (1.0 - lam_init)


def chunk_spatial_gating(u, v, w_s, b_s):
    B, S, G, C = u.shape
    n = -(-S // CHUNK_C)
    pad = n * CHUNK_C - S
    up = jnp.pad(u, ((0, 0), (0, pad), (0, 0), (0, 0))).reshape(B, n, CHUNK_C, G, C)
    vp = jnp.pad(v, ((0, 0), (0, pad), (0, 0), (0, 0))).reshape(B, n, CHUNK_C, G, C)
    causal = jnp.tril(jnp.ones((CHUNK_C, CHUNK_C), dtype=bool))
    w = jnp.where(causal, w_s, 0.0).astype(v.dtype)
    mixed = jnp.einsum('gts,bnsgc->bntgc', w, vp) + b_s.T[:, :, None]
    return (up * mixed).reshape(B, n * CHUNK_C, G, C)[:, :S]


def peer_ffn(h, w_q, sub_keys, expert_u, expert_v):
    B, S, D = h.shape
    T = B * S
    chunk = min(PEER_TCHUNK, T)
    n = -(-T // chunk)
    tp = jnp.pad(h.reshape(T, D), ((0, n * chunk - T), (0, 0))).reshape(n, chunk, D)

    def one(tc):
        q = jnp.dot(tc, w_q).reshape(chunk, PEER_HEADS, 2, D_KEY // 2)
        s = jnp.einsum('thpd,pnd->thpn', q, sub_keys).astype(jnp.float32)
        s1, i1 = lax.top_k(s[:, :, 0], PEER_TOPK)
        s2, i2 = lax.top_k(s[:, :, 1], PEER_TOPK)
        cand = (s1[..., :, None] + s2[..., None, :]).reshape(chunk, PEER_HEADS, PEER_TOPK * PEER_TOPK)
        cid = (i1[..., :, None] * N_KEYS + i2[..., None, :]).reshape(chunk, PEER_HEADS, PEER_TOPK * PEER_TOPK)
        best, pick = lax.top_k(cand, PEER_TOPK)
        eid = jnp.take_along_axis(cid, pick, axis=-1)
        g = jax.nn.softmax(best, axis=-1)
        act = jax.nn.gelu(jnp.einsum('td,thkd->thk', tc, expert_u[eid]))
        return jnp.einsum('thk,thkd->td', (g * act).astype(tc.dtype), expert_v[eid])

    out = lax.map(one, tp).reshape(n * chunk, D)[:T]
    return out.reshape(B, S, D)


def layer_step(x, c, q_start, past, lam_init, q_chunk_a, q_block_b,
               ada_w, ada_b, w_in, w_o, lam_q1, lam_k1, lam_q2, lam_k2, diff_gain,
               ln_c_g, ln_c_b, w_s, b_s, ln1_g, ln1_b, ln2_g, ln2_b,
               peer_wq, peer_subkeys, peer_u, peer_v):
    B, S, _ = x.shape
    mod = (jnp.dot(jax.nn.silu(c), ada_w) + ada_b)[:, None, :]
    sh1, sc1, g1, sh2, sc2, g2 = jnp.split(mod, 6, axis=-1)
    h = x * (1.0 + sc1) + sh1
    proj = jnp.dot(h, w_in)
    offsets = np.cumsum(IN_SIZES)[:-1].tolist()
    qa, ka, va, qb, kb, vb, uc, vc = jnp.split(proj, offsets, axis=-1)
    pos = q_start + jnp.arange(S, dtype=jnp.int32)
    qa = partial_rope(qa.reshape(B, S, H_A, DH_A), pos)
    ka = partial_rope(ka.reshape(B, S, H_A, DH_A), pos)
    va = va.reshape(B, S, H_A, DH_A)
    qb = partial_rope(qb.reshape(B, S, 2 * H_B, DK_B), pos).reshape(B, S, H_B, 2, DK_B)
    kb = partial_rope(kb.reshape(B, S, 2 * H_B, DK_B), pos).reshape(B, S, H_B, 2 * DK_B)
    vb = vb.reshape(B, S, H_B, DV_B)
    if past is None:
        ka_all, va_all, kb_all, vb_all = ka, va, kb, vb
    else:
        ka_all = jnp.concatenate([past[0], ka], axis=1)
        va_all = jnp.concatenate([past[1], va], axis=1)
        kb_all = jnp.concatenate([past[2], kb], axis=1)
        vb_all = jnp.concatenate([past[3], vb], axis=1)
    oa = moba_attention(qa, ka_all, va_all, q_start, q_chunk_a)
    lam = (jnp.exp(jnp.sum(lam_q1 * lam_k1).astype(jnp.float32))
           - jnp.exp(jnp.sum(lam_q2 * lam_k2).astype(jnp.float32)) + lam_init)
    ob = diff_attention(qb, kb_all.reshape(B, -1, H_B, 2, DK_B), vb_all, q_start, q_block_b,
                        lam, diff_gain, lam_init)
    ucg = jax.nn.gelu(uc).reshape(B, S, G_C, CG_C)
    vcn = layer_norm(jax.nn.gelu(vc), ln_c_g, ln_c_b)
    oc = chunk_spatial_gating(ucg, vcn.reshape(B, S, G_C, CG_C), w_s, b_s)
    mix = jnp.dot(jnp.concatenate([oa.reshape(B, S, W_A), ob.reshape(B, S, W_B),
                                   oc.reshape(B, S, W_C)], axis=-1), w_o)
    x = layer_norm(ALPHA * x + g1 * mix, ln1_g, ln1_b)
    h2 = x * (1.0 + sc2) + sh2
    ffn = peer_ffn(h2, peer_wq, peer_subkeys, peer_u, peer_v)
    x = layer_norm(ALPHA * x + g2 * ffn, ln2_g, ln2_b)
    return x, (ka, va, kb, vb, vcn)


def _stack(rows, i):
    return jnp.stack([r[i] for r in rows], axis=0)


def setup_inputs(seed: int = 0) -> dict:
    key = jax.random.key(seed)
    ks = jax.random.split(key, 32)
    f32 = jnp.float32
    D = D_MODEL
    n_pages = PAST_LEN // PAGE_SIZE
    n_used = DEC_BATCH * n_pages
    n_phys = n_used + max(1, n_used // 4)

    def nrm(k, shape, scale):
        return jax.random.normal(k, shape, f32) * scale

    page_table = jax.random.permutation(ks[0], n_phys)[:n_used].reshape(DEC_BATCH, n_pages).astype(jnp.int32)
    return {
        'x_prompt': nrm(ks[1], (BATCH, SEQ, D), 1.0),
        'x_sample': nrm(ks[2], (DEC_BATCH, DEC_SEQ, D), 1.0),
        'cache_a_k': nrm(ks[3], (DEPTH, n_phys, PAGE_SIZE, H_A, DH_A), 1.0),
        'cache_a_v': nrm(ks[4], (DEPTH, n_phys, PAGE_SIZE, H_A, DH_A), 1.0),
        'cache_b_k': nrm(ks[5], (DEPTH, n_phys, PAGE_SIZE, H_B, 2 * DK_B), 1.0),
        'cache_b_v': nrm(ks[6], (DEPTH, n_phys, PAGE_SIZE, H_B, DV_B), 1.0),
        'page_table': page_table,
        'c_prompt': nrm(ks[7], (BATCH, D), 1.0),
        'c_sample': nrm(ks[8], (DEC_BATCH, D), 1.0),
        'ada_w': nrm(ks[9], (DEPTH, D, 6 * D), 0.5 * D ** -0.5),
        'ada_b': nrm(ks[10], (DEPTH, 6 * D), 0.01),
        'w_in': nrm(ks[11], (DEPTH, D, IN_WIDTH), D ** -0.5),
        'w_o': nrm(ks[12], (DEPTH, MIX_WIDTH, D), BETA * MIX_WIDTH ** -0.5),
        'lam_q1': nrm(ks[13], (DEPTH, DK_B), 0.1),
        'lam_k1': nrm(ks[14], (DEPTH, DK_B), 0.1),
        'lam_q2': nrm(ks[15], (DEPTH, DK_B), 0.1),
        'lam_k2': nrm(ks[16], (DEPTH, DK_B), 0.1),
        'diff_gain': 1.0 + nrm(ks[17], (DEPTH, DV_B), 0.01),
        'ln_c_g': 1.0 + nrm(ks[18], (DEPTH, W_C), 0.01),
        'ln_c_b': nrm(ks[19], (DEPTH, W_C), 0.01),
        'w_s': nrm(ks[20], (DEPTH, G_C, CHUNK_C, CHUNK_C), CHUNK_C ** -0.5),
        'b_s': 1.0 + nrm(ks[21], (DEPTH, G_C, CHUNK_C), 0.01),
        'ln1_g': 1.0 + nrm(ks[22], (DEPTH, D), 0.01),
        'ln1_b': nrm(ks[23], (DEPTH, D), 0.01),
        'ln2_g': 1.0 + nrm(ks[24], (DEPTH, D), 0.01),
        'ln2_b': nrm(ks[25], (DEPTH, D), 0.01),
        'peer_wq': nrm(ks[26], (DEPTH, D, PEER_HEADS * D_KEY), D ** -0.5),
        'peer_subkeys': nrm(ks[27], (DEPTH, 2, N_KEYS, D_KEY // 2), (D_KEY // 2) ** -0.5),
        'peer_u': nrm(ks[28], (DEPTH, N_EXPERTS, D), D ** -0.5),
        'peer_v': nrm(ks[29], (DEPTH, N_EXPERTS, D), BETA * PEER_HEADS ** -0.5),
    }


def reference(x_prompt, x_sample, cache_a_k, cache_a_v, cache_b_k, cache_b_v, page_table,
              c_prompt, c_sample, ada_w, ada_b, w_in, w_o, lam_q1, lam_k1, lam_q2, lam_k2,
              diff_gain, ln_c_g, ln_c_b, w_s, b_s, ln1_g, ln1_b, ln2_g, ln2_b,
              peer_wq, peer_subkeys, peer_u, peer_v):
    past_len = page_table.shape[1] * cache_a_k.shape[2]
    dec_seq = x_sample.shape[1]
    xp, xs = x_prompt, x_sample
    rows_p, rows_s = [], []
    for l in range(DEPTH):
        lam_init = 0.8 - 0.6 * math.exp(-0.3 * l)
        lw = (ada_w[l], ada_b[l], w_in[l], w_o[l], lam_q1[l], lam_k1[l], lam_q2[l], lam_k2[l],
              diff_gain[l], ln_c_g[l], ln_c_b[l], w_s[l], b_s[l], ln1_g[l], ln1_b[l], ln2_g[l], ln2_b[l],
              peer_wq[l], peer_subkeys[l], peer_u[l], peer_v[l])
        xp, rp = layer_step(xp, c_prompt, 0, None, lam_init, MOBA_QCHUNK, DIFF_QBLOCK, *lw)
        past = (gather_pages(cache_a_k[l], page_table), gather_pages(cache_a_v[l], page_table),
                gather_pages(cache_b_k[l], page_table), gather_pages(cache_b_v[l], page_table))
        xs, rs = layer_step(xs, c_sample, past_len, past, lam_init, dec_seq, dec_seq, *lw)
        rows_p.append(rp)
        rows_s.append(rs)
    return (xp, xs,
            _stack(rows_p, 0), _stack(rows_p, 1), _stack(rows_p, 2), _stack(rows_p, 3),
            _stack(rows_s, 0), _stack(rows_s, 1), _stack(rows_s, 2), _stack(rows_s, 3), _stack(rows_s, 4))
```

```python
import functools
import math

import jax
import jax.numpy as jnp
from jax import lax
from jax.experimental import pallas as pl
from jax.experimental.pallas import tpu as pltpu

F32 = jnp.float32
BF16 = jnp.bfloat16
HIGHEST = lax.Precision.HIGHEST
NEG_INF = float("-inf")

LANES = 128
MOBA_BLOCK = 256
MOBA_TOPK = 3
PEER_TOPK = 16
ROPE_THETA = 500000.0
LN_EPS = 1e-5
SAMPLE_ROWS = 128
VMEM_MB = 1024 * 1024


def _params(n_axes, vmem_mb):
    return pltpu.CompilerParams(dimension_semantics=("arbitrary",) * n_axes,
                                vmem_limit_bytes=vmem_mb * VMEM_MB)


def _tile(n, preferred):
    t = min(n, preferred) // LANES * LANES
    while n % t:
        t -= LANES
    return t


def _nt(a, b, **kw):
    return lax.dot_general(a, b, (((1,), (1,)), ((), ())), preferred_element_type=F32, **kw)


def _ada_kernel(c_ref, w_ref, b_ref, o_ref):
    a = jax.nn.silu(c_ref[...]).astype(BF16)
    o_ref[...] = jnp.dot(a, w_ref[...].astype(BF16), preferred_element_type=F32) + b_ref[...]


def _ada_mods(c_all, ada_w, ada_b):
    depth, d, n = ada_w.shape
    rows = c_all.shape[0]
    tn = 512
    return pl.pallas_call(
        _ada_kernel, grid=(depth, n // tn),
        in_specs=[pl.BlockSpec((rows, d), lambda l, j: (0, 0)),
                  pl.BlockSpec((None, d, tn), lambda l, j: (l, 0, j)),
                  pl.BlockSpec((None, 1, tn), lambda l, j: (l, 0, j))],
        out_specs=pl.BlockSpec((None, rows, tn), lambda l, j: (l, 0, j)),
        out_shape=jax.ShapeDtypeStruct((depth, rows, n), F32),
        compiler_params=_params(2, 40), name="ada_mods",
    )(c_all, ada_w, ada_b.reshape(depth, 1, n))


class _Rows:
    def __init__(self, n_rows, tm, mods, rope_tab, seq):
        self.n_rows, self.tm, self.mods, self.rope_tab, self.seq = n_rows, tm, mods, rope_tab, seq
        self.per_row = mods.ndim == 4
        self.d = mods.shape[-1]

    def mod_spec(self, layer, k, tm=None):
        tm = tm or self.tm
        if self.per_row:
            return pl.BlockSpec((None, None, tm, self.d), lambda i: (layer, k, i, 0))
        seq = self.seq
        return pl.BlockSpec((None, None, None, 1, self.d), lambda i: (layer, (i * tm) // seq, k, 0, 0))

    def rope_spec(self, tm=None):
        tm = tm or self.tm
        w = self.rope_tab.shape[1]
        if self.per_row:
            return pl.BlockSpec((tm, w), lambda i: (i, 0))
        per_seq = self.seq // tm
        return pl.BlockSpec((tm, w), lambda i: (i % per_seq, 0))


def _modulate_kernel(x_ref, sc_ref, sh_ref, h_ref):
    h_ref[...] = (x_ref[...] * (1.0 + sc_ref[...]) + sh_ref[...]).astype(BF16)


def _modulate(rows, x, layer):
    tm, d = rows.tm, rows.d
    return pl.pallas_call(
        _modulate_kernel, grid=(rows.n_rows // tm,),
        in_specs=[pl.BlockSpec((tm, d), lambda i: (i, 0)), rows.mod_spec(layer, 1), rows.mod_spec(layer, 0)],
        out_specs=pl.BlockSpec((tm, d), lambda i: (i, 0)),
        out_shape=jax.ShapeDtypeStruct((rows.n_rows, d), BF16),
        compiler_params=_params(1, 32), name="modulate",
    )(x, rows.mods, rows.mods)


def _mm_kernel(*refs, ksizes):
    b_ref, o_ref = refs[len(ksizes)], refs[-1]
    acc, off = None, 0
    for a_ref, k in zip(refs, ksizes):
        part = jnp.dot(a_ref[...], b_ref[off:off + k, :], preferred_element_type=F32)
        acc = part if acc is None else acc + part
        off += k
    o_ref[...] = acc.astype(o_ref.dtype)


def _mm(a_list, b, layer, tm, tn, out_dtype, vmem_mb, name):
    m = a_list[0].shape[0]
    ks = tuple(a.shape[1] for a in a_list)
    k_all, n = b.shape[1], b.shape[2]
    assert sum(ks) == k_all and m % tm == 0 and n % tn == 0
    in_specs = [pl.BlockSpec((tm, k), lambda i, j: (i, 0)) for k in ks]
    in_specs.append(pl.BlockSpec((None, k_all, tn), lambda i, j: (layer, 0, j)))
    return pl.pallas_call(
        functools.partial(_mm_kernel, ksizes=ks), grid=(m // tm, n // tn), in_specs=in_specs,
        out_specs=pl.BlockSpec((tm, tn), lambda i, j: (i, j)),
        out_shape=jax.ShapeDtypeStruct((m, n), out_dtype),
        compiler_params=_params(2, vmem_mb), name=name,
    )(*a_list, b)


def _rope_tables(pos, dh_a, dk_b):
    lane = jnp.arange(LANES)

    def tabs(dh):
        r, half = dh // 4, dh // 8
        inv = ROPE_THETA ** (-jnp.arange(half, dtype=F32) * (2.0 / r))
        ang = pos.astype(F32)[:, None] * inv[None, :]
        cos, sin = jnp.cos(ang), jnp.sin(ang)
        ld = lane % dh
        idx = ld % half
        c = jnp.where(ld < r, cos[:, idx], 1.0)
        s_up = jnp.where((ld >= half) & (ld < r), sin[:, idx], 0.0)
        s_dn = jnp.where(ld < half, -sin[:, idx], 0.0)
        return [c, s_up, s_dn]

    return jnp.concatenate(tabs(dh_a) + tabs(dk_b), axis=1).astype(F32)


def _post_kernel(p_ref, tab_ref, lcg_ref, lcb_ref, qa_ref, ka_ref, va_ref, qb_ref, kb_ref, vb_ref, uc_ref, vc_ref,
                 *, w_a, w_b, w_c, half_a, half_b):
    tab = tab_ref[...]
    t = [tab[:, i * LANES:(i + 1) * LANES] for i in range(6)]

    def rope(x, c, s_up, s_dn, half):
        return x * c + pltpu.roll(x, half, 1) * s_up + pltpu.roll(x, LANES - half, 1) * s_dn

    def rope_cols(src_off, dst_ref, width, tabs, half):
        for j in range(width // LANES):
            x = p_ref[:, src_off + j * LANES: src_off + (j + 1) * LANES]
            dst_ref[:, j * LANES:(j + 1) * LANES] = rope(x, *tabs, half).astype(dst_ref.dtype)

    off = 0
    rope_cols(off, qa_ref, w_a, t[0:3], half_a); off += w_a
    rope_cols(off, ka_ref, w_a, t[0:3], half_a); off += w_a
    va_ref[...] = p_ref[:, off:off + w_a]; off += w_a
    rope_cols(off, qb_ref, w_b, t[3:6], half_b); off += w_b
    rope_cols(off, kb_ref, w_b, t[3:6], half_b); off += w_b
    vb_ref[...] = p_ref[:, off:off + w_b]; off += w_b
    uc_ref[...] = jax.nn.gelu(p_ref[:, off:off + w_c]); off += w_c
    g = jax.nn.gelu(p_ref[:, off:off + w_c])
    mu = jnp.mean(g, axis=-1, keepdims=True)
    gc = g - mu
    var = jnp.mean(gc * gc, axis=-1, keepdims=True)
    vc_ref[...] = gc * lax.rsqrt(var + LN_EPS) * lcg_ref[...] + lcb_ref[...]


def _post(rows, proj, ln_c_g, ln_c_b, layer, w_a, w_b, w_c, dh_a, dk_b):
    tm, n = rows.tm, rows.n_rows
    depth = ln_c_g.shape[0]
    widths = (w_a, w_a, w_a, w_b, w_b, w_b, w_c, w_c)
    dtypes = (BF16, F32, F32, BF16, F32, F32, F32, F32)
    row_spec = lambda w: pl.BlockSpec((tm, w), lambda i: (i, 0))
    return pl.pallas_call(
        functools.partial(_post_kernel, w_a=w_a, w_b=w_b, w_c=w_c, half_a=dh_a // 8, half_b=dk_b // 8),
        grid=(n // tm,),
        in_specs=[row_spec(proj.shape[1]), rows.rope_spec(),
                  pl.BlockSpec((None, 1, w_c), lambda i: (layer, 0, 0)),
                  pl.BlockSpec((None, 1, w_c), lambda i: (layer, 0, 0))],
        out_specs=[row_spec(w) for w in widths],
        out_shape=[jax.ShapeDtypeStruct((n, w), dt) for w, dt in zip(widths, dtypes)],
        compiler_params=_params(1, 48), name="proj_split",
    )(proj, rows.rope_tab, ln_c_g.reshape(depth, 1, w_c), ln_c_b.reshape(depth, 1, w_c))


def _moba_kernel(q_ref, k_ref, v_ref, o_ref, km_ref, *, nb, scale):
    blk = MOBA_BLOCK
    qi = pl.program_id(2)

    @pl.when(qi == 0)
    def _():
        km_ref[...] = jnp.zeros_like(km_ref)
        for j in range(nb):
            km_ref[j:j + 1, :] = jnp.mean(k_ref[j * blk:(j + 1) * blk, :], axis=0, keepdims=True)

    q = q_ref[...]
    gate = _nt(q.astype(F32), km_ref[...], precision=HIGHEST)
    lane = lax.broadcasted_iota(jnp.int32, (blk, LANES), 1)
    g = jnp.where(lane < qi, gate, NEG_INF)
    rank = jnp.zeros((blk, LANES), F32)
    for m in range(nb):
        gm = g[:, m:m + 1]
        tie_first = jnp.where(lane > m, 1.0, 0.0)
        rank = rank + jnp.where(gm > g, 1.0, jnp.where(gm == g, tie_first, 0.0))
    sel = jnp.where(lane < qi, jnp.where(rank < MOBA_TOPK, 1.0, 0.0), 0.0)

    def block(j):
        start = pl.multiple_of(j * blk, blk)
        kj = k_ref[pl.ds(start, blk), :].astype(BF16)
        vj = v_ref[pl.ds(start, blk), :].astype(BF16)
        return _nt(q, kj) * scale, vj

    s, vj = block(qi)
    row = lax.broadcasted_iota(jnp.int32, (blk, blk), 0)
    col = lax.broadcasted_iota(jnp.int32, (blk, blk), 1)
    s = jnp.where(col <= row, s, NEG_INF)
    m0 = jnp.max(s, axis=1, keepdims=True)
    p = jnp.exp(s - m0)
    l0 = jnp.sum(p, axis=1, keepdims=True)
    acc0 = jnp.dot(p.astype(BF16), vj, preferred_element_type=F32)

    def body(j, carry):
        m_run, l_run, acc = carry
        s, vj = block(j)
        chosen = jnp.max(jnp.where(lane == j, sel, 0.0), axis=1, keepdims=True)
        s = jnp.where(chosen > 0.0, s, NEG_INF)
        m_new = jnp.maximum(m_run, jnp.max(s, axis=1, keepdims=True))
        alpha = jnp.exp(m_run - m_new)
        p = jnp.exp(s - m_new)
        l_new = alpha * l_run + jnp.sum(p, axis=1, keepdims=True)
        acc = alpha * acc + jnp.dot(p.astype(BF16), vj, preferred_element_type=F32)
        return m_new, l_new, acc

    _, l_fin, acc = lax.fori_loop(0, qi, body, (m0, l0, acc0))
    o_ref[...] = (acc / l_fin).astype(o_ref.dtype)


def _moba_prompt(q, k, v, batch, seq, heads):
    nb = seq // MOBA_BLOCK
    assert seq % MOBA_BLOCK == 0 and nb <= LANES
    return pl.pallas_call(
        functools.partial(_moba_kernel, nb=nb, scale=LANES ** -0.5),
        grid=(batch, heads, nb),
        in_specs=[pl.BlockSpec((MOBA_BLOCK, LANES), lambda b, h, i: (b * nb + i, h)),
                  pl.BlockSpec((seq, LANES), lambda b, h, i: (b, h)),
                  pl.BlockSpec((seq, LANES), lambda b, h, i: (b, h))],
        out_specs=pl.BlockSpec((MOBA_BLOCK, LANES), lambda b, h, i: (b * nb + i, h)),
        out_shape=jax.ShapeDtypeStruct(q.shape, BF16),
        scratch_shapes=[pltpu.VMEM((LANES, LANES), F32)],
        compiler_params=_params(3, 32), name="moba_prompt",
    )(q, k, v)


def _lam(lamp_ref, lam_init):
    lp = lamp_ref[...]
    d1 = jnp.sum(lp[0:1] * lp[1:2], axis=1, keepdims=True)
    d2 = jnp.sum(lp[2:3] * lp[3:4], axis=1, keepdims=True)
    return jnp.exp(d1) - jnp.exp(d2) + lam_init


def _diff_finish(o, gain, lam_init):
    ms = jnp.mean(o * o, axis=-1, keepdims=True)
    return o * lax.rsqrt(ms + LN_EPS) * gain * (1.0 - lam_init)


def _diff_kernel(lamp_ref, gain_ref, q_ref, k_ref, v_ref, o_ref, *, tq, dk, lam_init):
    qi = pl.program_id(2)
    scale = dk ** -0.5
    lam = _lam(lamp_ref, lam_init)
    q = q_ref[...]
    lane = lax.broadcasted_iota(jnp.int32, (tq, LANES), 1)
    zero = jnp.zeros_like(q)
    q1 = jnp.where(lane < dk, q, zero)
    q2 = jnp.where(lane >= dk, q, zero)

    def scores(j):
        start = pl.multiple_of(j * tq, tq)
        kj = k_ref[pl.ds(start, tq), :].astype(BF16)
        vj = v_ref[pl.ds(start, tq), :].astype(BF16)
        return _nt(q1, kj) * scale, _nt(q2, kj) * scale, vj

    def first(s, vj):
        m = jnp.max(s, axis=1, keepdims=True)
        p = jnp.exp(s - m)
        return m, jnp.sum(p, axis=1, keepdims=True), jnp.dot(p.astype(BF16), vj, preferred_element_type=F32)

    def update(state, s, vj):
        m_run, l_run, acc = state
        m_new = jnp.maximum(m_run, jnp.max(s, axis=1, keepdims=True))
        alpha = jnp.exp(m_run - m_new)
        p = jnp.exp(s - m_new)
        return (m_new, alpha * l_run + jnp.sum(p, axis=1, keepdims=True),
                alpha * acc + jnp.dot(p.astype(BF16), vj, preferred_element_type=F32))

    s1, s2, vj = scores(qi)
    row = lax.broadcasted_iota(jnp.int32, (tq, tq), 0)
    col = lax.broadcasted_iota(jnp.int32, (tq, tq), 1)
    causal = col <= row
    st1 = first(jnp.where(causal, s1, NEG_INF), vj)
    st2 = first(jnp.where(causal, s2, NEG_INF), vj)

    def body(j, carry):
        a, b = carry
        s1, s2, vj = scores(j)
        return update(a, s1, vj), update(b, s2, vj)

    st1, st2 = lax.fori_loop(0, qi, body, (st1, st2))
    o = st1[2] / st1[1] - lam * (st2[2] / st2[1])
    o_ref[...] = _diff_finish(o, gain_ref[...], lam_init).astype(o_ref.dtype)


def _diff_prompt(q, k, v, lamp, gain, layer, lam_init, batch, seq, heads, dk):
    tq = 256
    nq = seq // tq
    depth = lamp.shape[0]
    return pl.pallas_call(
        functools.partial(_diff_kernel, tq=tq, dk=dk, lam_init=lam_init),
        grid=(batch, heads, nq),
        in_specs=[pl.BlockSpec((None, 4, dk), lambda b, h, i: (layer, 0, 0)),
                  pl.BlockSpec((None, 1, LANES), lambda b, h, i: (layer, 0, 0)),
                  pl.BlockSpec((tq, LANES), lambda b, h, i: (b * nq + i, h)),
                  pl.BlockSpec((seq, LANES), lambda b, h, i: (b, h)),
                  pl.BlockSpec((seq, LANES), lambda b, h, i: (b, h))],
        out_specs=pl.BlockSpec((tq, LANES), lambda b, h, i: (b * nq + i, h)),
        out_shape=jax.ShapeDtypeStruct(q.shape, BF16),
        compiler_params=_params(3, 32), name="diff_prompt",
    )(lamp, gain.reshape(depth, 1, LANES), q, k, v)


def _gmlp_kernel(u_ref, v_ref, w_ref, bt_ref, o_ref, *, groups, chunk):
    row = lax.broadcasted_iota(jnp.int32, (chunk, chunk), 0)
    col = lax.broadcasted_iota(jnp.int32, (chunk, chunk), 1)
    causal = col <= row
    for g in range(groups):
        sl = slice(g * LANES, (g + 1) * LANES)
        w = jnp.where(causal, w_ref[g], 0.0).astype(BF16)
        mixed = jnp.dot(w, v_ref[:, sl].astype(BF16), preferred_element_type=F32) + bt_ref[:, g:g + 1]
        o_ref[:, sl] = (u_ref[:, sl] * mixed).astype(o_ref.dtype)


def _gmlp_prompt(u, v, w_s, b_st, layer):
    n, w_c = u.shape
    groups, chunk = w_s.shape[1], w_s.shape[2]
    return pl.pallas_call(
        functools.partial(_gmlp_kernel, groups=groups, chunk=chunk), grid=(n // chunk,),
        in_specs=[pl.BlockSpec((chunk, w_c), lambda i: (i, 0)), pl.BlockSpec((chunk, w_c), lambda i: (i, 0)),
                  pl.BlockSpec((None, groups, chunk, chunk), lambda i: (layer, 0, 0, 0)),
                  pl.BlockSpec((None, chunk, groups), lambda i: (layer, 0, 0))],
        out_specs=pl.BlockSpec((chunk, w_c), lambda i: (i, 0)),
        out_shape=jax.ShapeDtypeStruct((n, w_c), BF16),
        compiler_params=_params(1, 32), name="gmlp_prompt",
    )(u, v, w_s, b_st)


def _gmlp_first_kernel(u_ref, v_ref, w_ref, bt_ref, o_ref, *, groups):
    for g in range(groups):
        sl = slice(g * LANES, (g + 1) * LANES)
        mixed = w_ref[g, 0:1, 0:1] * v_ref[:, sl] + bt_ref[0:1, g:g + 1]
        o_ref[:, sl] = (u_ref[:, sl] * mixed).astype(o_ref.dtype)


def _gmlp_sample(u, v, w_s, b_st, layer):
    n, w_c = u.shape
    groups, chunk = w_s.shape[1], w_s.shape[2]
    return pl.pallas_call(
        functools.partial(_gmlp_first_kernel, groups=groups), grid=(1,),
        in_specs=[pl.BlockSpec((n, w_c), lambda i: (0, 0)), pl.BlockSpec((n, w_c), lambda i: (0, 0)),
                  pl.BlockSpec((None, groups, chunk, chunk), lambda i: (layer, 0, 0, 0)),
                  pl.BlockSpec((None, chunk, groups), lambda i: (layer, 0, 0))],
        out_specs=pl.BlockSpec((n, w_c), lambda i: (0, 0)),
        out_shape=jax.ShapeDtypeStruct((n, w_c), BF16),
        compiler_params=_params(1, 32), name="gmlp_sample",
    )(u, v, w_s, b_st)


def _ln_kernel(*refs, alpha, emit_h):
    x_ref, y_ref, gate_ref, g_ref, b_ref = refs[:5]
    z = alpha * x_ref[...] + gate_ref[...] * y_ref[...]
    mu = jnp.mean(z, axis=-1, keepdims=True)
    zc = z - mu
    var = jnp.mean(zc * zc, axis=-1, keepdims=True)
    xn = zc * lax.rsqrt(var + LN_EPS) * g_ref[...] + b_ref[...]
    if emit_h:
        sc_ref, sh_ref, xo_ref, h_ref = refs[5:]
        h_ref[...] = (xn * (1.0 + sc_ref[...]) + sh_ref[...]).astype(BF16)
    else:
        xo_ref = refs[5]
    xo_ref[...] = xn


def _ln(rows, x, y, ln_g, ln_b, layer, gate_k, alpha, next_mod=None):
    tm, d, n = min(rows.tm, 256), rows.d, rows.n_rows
    depth = ln_g.shape[0]
    row_spec = pl.BlockSpec((tm, d), lambda i: (i, 0))
    par_spec = pl.BlockSpec((None, 1, d), lambda i: (layer, 0, 0))
    in_specs = [row_spec, row_spec, rows.mod_spec(layer, gate_k, tm), par_spec, par_spec]
    args = [x, y, rows.mods, ln_g.reshape(depth, 1, d), ln_b.reshape(depth, 1, d)]
    out_specs, out_shape = [row_spec], [jax.ShapeDtypeStruct((n, d), F32)]
    if next_mod is not None:
        nl, k_sc, k_sh = next_mod
        in_specs += [rows.mod_spec(nl, k_sc, tm), rows.mod_spec(nl, k_sh, tm)]
        args += [rows.mods, rows.mods]
        out_specs.append(row_spec)
        out_shape.append(jax.ShapeDtypeStruct((n, d), BF16))
    out = pl.pallas_call(
        functools.partial(_ln_kernel, alpha=alpha, emit_h=next_mod is not None), grid=(n // tm,),
        in_specs=in_specs, out_specs=out_specs, out_shape=out_shape,
        compiler_params=_params(1, 48), name="deepnorm_ln",
    )(*args)
    return (out[0], out[1]) if next_mod is not None else (out[0], None)


def _peer_select_kernel(q_ref, keys_ref, s1_ref, e1_ref, s2_ref, e2_ref, tau_ref, cand_ref, *, heads, n_keys, pairs):
    tm = q_ref.shape[0]
    sub = lax.broadcasted_iota(jnp.int32, (n_keys, tm), 0)
    csub = lax.broadcasted_iota(jnp.int32, cand_ref.shape, 0)

    def pop_max(x, iota, big):
        m = jnp.max(x, axis=0, keepdims=True)
        first = jnp.min(jnp.where(x == m, iota, big), axis=0, keepdims=True)
        return m, jnp.where(iota == first, NEG_INF, x)

    def top_rows(s):
        rows, x = [], s
        for _ in range(PEER_TOPK):
            m, x = pop_max(x, sub, n_keys)
            rows.append(m)
        return rows

    for h in range(heads):
        s1 = _nt(keys_ref[0], q_ref[:, (2 * h) * n_keys:(2 * h + 1) * n_keys], precision=HIGHEST)
        s2 = _nt(keys_ref[1], q_ref[:, (2 * h + 1) * n_keys:(2 * h + 2) * n_keys], precision=HIGHEST)
        a1, a2 = top_rows(s1), top_rows(s2)
        cand_ref[...] = jnp.full(cand_ref.shape, NEG_INF, F32)
        for r, (i, j) in enumerate(pairs):
            cand_ref[r:r + 1, :] = a1[i] + a2[j]
        c = cand_ref[...]
        x, tau = c, None
        for _ in range(PEER_TOPK):
            tau, x = pop_max(x, csub, cand_ref.shape[0])
        cmax = a1[0] + a2[0]
        z = jnp.sum(jnp.where(c >= tau, jnp.exp(c - cmax), 0.0), axis=0, keepdims=True)
        s1_ref[h] = s1
        s2_ref[h] = s2
        e1_ref[h] = jnp.exp(s1 - a1[0]) / z
        e2_ref[h] = jnp.exp(s2 - a2[0])
        tau_ref[h] = tau


def _peer_select(q, keys, layer, heads, tm):
    n, n_keys = q.shape[0], keys.shape[2]
    pairs = tuple((i, j) for i in range(PEER_TOPK) for j in range(PEER_TOPK) if (i + 1) * (j + 1) <= PEER_TOPK)
    n_cand = -(-len(pairs) // 8) * 8
    big = pl.BlockSpec((heads, n_keys, tm), lambda i: (0, 0, i))
    big_shape = jax.ShapeDtypeStruct((heads, n_keys, n), F32)
    return pl.pallas_call(
        functools.partial(_peer_select_kernel, heads=heads, n_keys=n_keys, pairs=pairs), grid=(n // tm,),
        in_specs=[pl.BlockSpec((tm, q.shape[1]), lambda i: (i, 0)),
                  pl.BlockSpec((None, 2, n_keys, keys.shape[3]), lambda i: (layer, 0, 0, 0))],
        out_specs=[big, big, big, big, pl.BlockSpec((heads, 1, tm), lambda i: (0, 0, i))],
        out_shape=[big_shape] * 4 + [jax.ShapeDtypeStruct((heads, 1, n), F32)],
        scratch_shapes=[pltpu.VMEM((n_cand, tm), F32)],
        compiler_params=_params(1, 32), name="peer_select",
    )(q, keys)


def _peer_dense_kernel(h_ref, u_ref, v_ref, s1_ref, e1_ref, s2_ref, e2_ref, tau_ref, o_ref, w_ref,
                       *, heads, n_keys, n1_per_step):
    j = pl.program_id(1)
    st = _nt(u_ref[...], h_ref[...])
    for r in range(n1_per_step):
        n1 = j * n1_per_step + r
        g = None
        for h in range(heads):
            s1 = s1_ref[h, pl.ds(n1, 1), :]
            e1 = e1_ref[h, pl.ds(n1, 1), :]
            term = e1 * jnp.where(s1 + s2_ref[h] >= tau_ref[h], e2_ref[h], 0.0)
            g = term if g is None else g + term
        act = jax.nn.gelu(st[r * n_keys:(r + 1) * n_keys, :])
        w_ref[r * n_keys:(r + 1) * n_keys, :] = (g * act).astype(BF16)
    part = lax.dot_general(w_ref[...], v_ref[...], (((0,), (0,)), ((), ())), preferred_element_type=F32)

    @pl.when(j == 0)
    def _():
        o_ref[...] = part

    @pl.when(j > 0)
    def _():
        o_ref[...] += part


def _peer_dense(h2, u, v, sel, layer, heads, tm, n1_per_step, vmem_mb):
    n, d = h2.shape
    n_keys = sel[0].shape[1]
    et = n1_per_step * n_keys
    n_steps = n_keys // n1_per_step
    resident = pl.BlockSpec((heads, n_keys, tm), lambda i, j: (0, 0, i))
    return pl.pallas_call(
        functools.partial(_peer_dense_kernel, heads=heads, n_keys=n_keys, n1_per_step=n1_per_step),
        grid=(n // tm, n_steps),
        in_specs=[pl.BlockSpec((tm, d), lambda i, j: (i, 0)),
                  pl.BlockSpec((None, et, d), lambda i, j: (layer, j, 0)),
                  pl.BlockSpec((None, et, d), lambda i, j: (layer, j, 0)),
                  resident, resident, resident, resident,
                  pl.BlockSpec((heads, 1, tm), lambda i, j: (0, 0, i))],
        out_specs=pl.BlockSpec((tm, d), lambda i, j: (i, 0)),
        out_shape=jax.ShapeDtypeStruct((n, d), F32),
        scratch_shapes=[pltpu.VMEM((et, tm), BF16)],
        compiler_params=_params(2, vmem_mb), name="peer_dense",
    )(h2, u, v, *sel)


def _moba_pick_kernel(pt_ref, q_ref, k_ref, sel_ref, ks_ref, *, heads, pages_per_block, n_pages):
    del pt_ref
    p = pl.program_id(1)
    blk = p // pages_per_block
    psum = jnp.sum(k_ref[...], axis=0, keepdims=True)

    @pl.when(p % pages_per_block == 0)
    def _():
        ks_ref[pl.ds(blk, 1), :] = psum

    @pl.when(p % pages_per_block != 0)
    def _():
        ks_ref[pl.ds(blk, 1), :] += psum

    @pl.when(p == n_pages - 1)
    def _():
        nb = ks_ref.shape[0]
        prod = ks_ref[...] * (1.0 / MOBA_BLOCK) * q_ref[...].astype(F32)
        sub = lax.broadcasted_iota(jnp.int32, (nb, 1), 0)
        lane = lax.broadcasted_iota(jnp.int32, (1, LANES), 1)
        out = jnp.zeros((1, LANES), jnp.int32)
        for h in range(heads):
            g = jnp.sum(prod[:, h * LANES:(h + 1) * LANES], axis=1, keepdims=True)
            for t in range(MOBA_TOPK):
                m = jnp.max(g, axis=0, keepdims=True)
                first = jnp.min(jnp.where(g == m, sub, nb), axis=0, keepdims=True)
                out = jnp.where(lane == h * MOBA_TOPK + t, first, out)
                g = jnp.where(sub == first, NEG_INF, g)
        sel_ref[...] = out


def _moba_pick(page_table, q3, cache_k, layer, heads):
    batch, n_pages = page_table.shape
    page, w = cache_k.shape[2], cache_k.shape[3]
    ppb = MOBA_BLOCK // page
    nb = n_pages // ppb
    assert n_pages % ppb == 0 and nb >= MOBA_TOPK and heads * MOBA_TOPK <= LANES
    return pl.pallas_call(
        functools.partial(_moba_pick_kernel, heads=heads, pages_per_block=ppb, n_pages=n_pages),
        grid_spec=pltpu.PrefetchScalarGridSpec(
            num_scalar_prefetch=1, grid=(batch, n_pages),
            in_specs=[pl.BlockSpec((None, 1, w), lambda b, p, pt: (b, 0, 0)),
                      pl.BlockSpec((None, None, page, w), lambda b, p, pt: (layer, pt[b, p], 0, 0))],
            out_specs=pl.BlockSpec((None, 1, LANES), lambda b, p, pt: (b, 0, 0)),
            scratch_shapes=[pltpu.VMEM((nb, w), F32)]),
        out_shape=jax.ShapeDtypeStruct((batch, 1, LANES), jnp.int32),
        compiler_params=_params(2, 32), name="moba_pick",
    )(page_table, q3, cache_k)


def _moba_sample_kernel(pt_ref, sel_ref, q_ref, kn_ref, vn_ref, k_ref, v_ref, o_ref, m_ref, l_ref, acc_ref,
                        *, n_steps, scale):
    del pt_ref, sel_ref
    s_id = pl.program_id(2)
    q = q_ref[...]

    @pl.when(s_id == 0)
    def _():
        m_ref[...] = jnp.sum(q.astype(F32) * kn_ref[...], axis=1, keepdims=True) * scale
        l_ref[...] = jnp.ones_like(l_ref)
        acc_ref[...] = vn_ref[...]

    q16 = jnp.broadcast_to(q, (16, LANES))
    s = _nt(q16, k_ref[...].astype(BF16))[0:1, :] * scale
    m_run = m_ref[...]
    m_new = jnp.maximum(m_run, jnp.max(s, axis=1, keepdims=True))
    alpha = jnp.exp(m_run - m_new)
    p = jnp.exp(s - m_new)
    l_new = alpha * l_ref[...] + jnp.sum(p, axis=1, keepdims=True)
    p16 = jnp.broadcast_to(p.astype(BF16), (16, p.shape[1]))
    pv = jnp.dot(p16, v_ref[...].astype(BF16), preferred_element_type=F32)[0:1, :]
    acc = alpha * acc_ref[...] + pv
    m_ref[...] = m_new
    l_ref[...] = l_new
    acc_ref[...] = acc

    @pl.when(s_id == n_steps - 1)
    def _():
        o_ref[...] = (acc / l_new).astype(o_ref.dtype)


def _moba_sample(page_table, sel_flat, q3, kn3, vn3, cache_k, cache_v, layer, heads):
    batch, n_pages = page_table.shape
    page = cache_k.shape[2]
    ppb = MOBA_BLOCK // page
    n_steps = MOBA_TOPK * ppb

    def page_map(b, h, s, pt, sel):
        return (layer, pt[b, sel[b * LANES + h * MOBA_TOPK + s // ppb] * ppb + s % ppb], 0, h)

    row = lambda b, h, s, pt, sel: (b, 0, h)
    return pl.pallas_call(
        functools.partial(_moba_sample_kernel, n_steps=n_steps, scale=LANES ** -0.5),
        grid_spec=pltpu.PrefetchScalarGridSpec(
            num_scalar_prefetch=2, grid=(batch, heads, n_steps),
            in_specs=[pl.BlockSpec((None, 1, LANES), row), pl.BlockSpec((None, 1, LANES), row),
                      pl.BlockSpec((None, 1, LANES), row),
                      pl.BlockSpec((None, None, page, LANES), page_map),
                      pl.BlockSpec((None, None, page, LANES), page_map)],
            out_specs=pl.BlockSpec((None, 1, LANES), row),
            scratch_shapes=[pltpu.VMEM((1, 1), F32), pltpu.VMEM((1, 1), F32), pltpu.VMEM((1, LANES), F32)]),
        out_shape=jax.ShapeDtypeStruct((batch, 1, heads * LANES), BF16),
        compiler_params=_params(3, 32), name="moba_sample",
    )(page_table, sel_flat, q3, kn3, vn3, cache_k, cache_v)


def _diff_sample_kernel(pt_ref, lamp_ref, gain_ref, q_ref, kn_ref, vn_ref, k_ref, v_ref, o_ref, m_ref, l_ref, acc_ref,
                        *, heads, dk, n_pages, lam_init):
    del pt_ref
    p_id = pl.program_id(1)
    scale = dk ** -0.5
    w = heads * LANES
    n_maps = 2 * heads
    qf = q_ref[...].astype(F32)
    seg = jnp.where(lax.broadcasted_iota(jnp.int32, (w, LANES), 0) // dk
                    == lax.broadcasted_iota(jnp.int32, (w, LANES), 1), 1.0, 0.0)
    lane = lax.broadcasted_iota(jnp.int32, (1, LANES), 1)

    def map_scores(k):
        return jnp.dot(k * qf, seg, precision=HIGHEST, preferred_element_type=F32) * scale

    def accumulate(s, v, first):
        s = jnp.where(lane < n_maps, s, 0.0)
        m_blk = jnp.max(s, axis=0, keepdims=True)
        if first:
            m_new, alpha = m_blk, jnp.zeros_like(m_blk)
        else:
            m_run = m_ref[...]
            m_new = jnp.maximum(m_run, m_blk)
            alpha = jnp.exp(m_run - m_new)
        p = jnp.exp(s - m_new)
        psum = jnp.sum(p, axis=0, keepdims=True)
        l_ref[...] = psum if first else alpha * l_ref[...] + psum
        m_ref[...] = m_new
        for j in range(n_maps):
            hv = v[:, (j // 2) * LANES:(j // 2 + 1) * LANES]
            contrib = jnp.sum(p[:, j:j + 1] * hv, axis=0, keepdims=True)
            if first:
                acc_ref[j:j + 1, :] = contrib
            else:
                acc_ref[j:j + 1, :] = alpha[:, j:j + 1] * acc_ref[j:j + 1, :] + contrib

    @pl.when(p_id == 0)
    def _():
        k8 = jnp.broadcast_to(kn_ref[...], (8, w))
        s = map_scores(k8)
        s = jnp.where(lax.broadcasted_iota(jnp.int32, (8, LANES), 0) == 0, s, NEG_INF)
        accumulate(s, jnp.broadcast_to(vn_ref[...], (8, w)), True)

    accumulate(map_scores(k_ref[...]), v_ref[...], False)

    @pl.when(p_id == n_pages - 1)
    def _():
        lam = _lam(lamp_ref, lam_init)
        l = l_ref[...]
        for h in range(heads):
            o = (acc_ref[2 * h:2 * h + 1, :] / l[:, 2 * h:2 * h + 1]
                 - lam * (acc_ref[2 * h + 1:2 * h + 2, :] / l[:, 2 * h + 1:2 * h + 2]))
            o_ref[:, h * LANES:(h + 1) * LANES] = _diff_finish(o, gain_ref[...], lam_init).astype(o_ref.dtype)


def _diff_sample(page_table, q3, kn3, vn3, cache_k, cache_v, lamp, gain, layer, lam_init, heads, dk):
    batch, n_pages = page_table.shape
    page, w = cache_k.shape[2], cache_k.shape[3]
    depth = lamp.shape[0]
    row = lambda b, p, pt: (b, 0, 0)
    page_map = lambda b, p, pt: (layer, pt[b, p], 0, 0)
    return pl.pallas_call(
        functools.partial(_diff_sample_kernel, heads=heads, dk=dk, n_pages=n_pages, lam_init=lam_init),
        grid_spec=pltpu.PrefetchScalarGridSpec(
            num_scalar_prefetch=1, grid=(batch, n_pages),
            in_specs=[pl.BlockSpec((None, 4, dk), lambda b, p, pt: (layer, 0, 0)),
                      pl.BlockSpec((None, 1, LANES), lambda b, p, pt: (layer, 0, 0)),
                      pl.BlockSpec((None, 1, w), row), pl.BlockSpec((None, 1, w), row), pl.BlockSpec((None, 1, w), row),
                      pl.BlockSpec((None, None, page, w), page_map),
                      pl.BlockSpec((None, None, page, w), page_map)],
            out_specs=pl.BlockSpec((None, 1, w), row),
            scratch_shapes=[pltpu.VMEM((1, LANES), F32), pltpu.VMEM((1, LANES), F32),
                            pltpu.VMEM((2 * heads, LANES), F32)]),
        out_shape=jax.ShapeDtypeStruct((batch, 1, w), BF16),
        compiler_params=_params(2, 32), name="diff_sample",
    )(page_table, lamp, gain.reshape(depth, 1, LANES), q3, kn3, vn3, cache_k, cache_v)


def kernel(x_prompt, x_sample, cache_a_k, cache_a_v, cache_b_k, cache_b_v, page_table, c_prompt, c_sample, ada_w, ada_b, w_in, w_o, lam_q1, lam_k1, lam_q2, lam_k2, diff_gain, ln_c_g, ln_c_b, w_s, b_s, ln1_g, ln1_b, ln2_g, ln2_b, peer_wq, peer_subkeys, peer_u, peer_v):
    batch, seq, d = x_prompt.shape
    dec_batch, dec_seq, _ = x_sample.shape
    depth = w_in.shape[0]
    n_phys, page, h_a, dh_a = cache_a_k.shape[1:]
    h_b, dv_b = cache_b_v.shape[3:]
    dk_b = cache_b_k.shape[4] // 2
    w_a, w_b, w_c = h_a * dh_a, h_b * dv_b, ln_c_g.shape[1]
    n_keys = peer_subkeys.shape[2]
    peer_heads = peer_wq.shape[2] // (2 * peer_subkeys.shape[3])
    past_len = page_table.shape[1] * page
    assert dec_seq == 1 and dh_a == LANES and dv_b == LANES and n_keys == LANES and peer_subkeys.shape[3] == LANES
    assert past_len % MOBA_BLOCK == 0 and w_s.shape[2] == LANES and dec_batch <= SAMPLE_ROWS
    alpha = (2 * depth) ** 0.25
    n_p, n_s = batch * seq, SAMPLE_ROWS

    w_in_b, w_o_b, wq_b = w_in.astype(BF16), w_o.astype(BF16), peer_wq.astype(BF16)
    u_b, v_b = peer_u.astype(BF16), peer_v.astype(BF16)
    lamp = jnp.stack([lam_q1, lam_k1, lam_q2, lam_k2], axis=1)
    b_st = jnp.swapaxes(b_s, 1, 2)
    ck_a = cache_a_k.reshape(depth, n_phys, page, w_a)
    cv_a = cache_a_v.reshape(depth, n_phys, page, w_a)
    ck_b = cache_b_k.reshape(depth, n_phys, page, w_b)
    cv_b = cache_b_v.reshape(depth, n_phys, page, w_b)

    c_rows = -(-(batch + dec_batch) // 16) * 16
    c_all = jnp.zeros((c_rows, d), F32).at[:batch].set(c_prompt).at[batch:batch + dec_batch].set(c_sample)
    mods = _ada_mods(c_all, ada_w, ada_b)
    mods_p = mods[:, :batch].reshape(depth, batch, 6, 1, d)
    mods_s = mods[:, batch:batch + dec_batch].reshape(depth, dec_batch, 6, d).transpose(0, 2, 1, 3)
    mods_s = jnp.pad(mods_s, ((0, 0), (0, 0), (0, n_s - dec_batch), (0, 0)))

    rows_p = _Rows(n_p, 128, mods_p, _rope_tables(jnp.arange(seq), dh_a, dk_b), seq)
    rows_s = _Rows(n_s, n_s, mods_s, _rope_tables(jnp.full((n_s,), past_len), dh_a, dk_b), 1)

    x_p = x_prompt.reshape(n_p, d)
    x_s = jnp.pad(x_sample.reshape(dec_batch, d), ((0, n_s - dec_batch), (0, 0)))
    h_p = _modulate(rows_p, x_p, 0)
    h_s = _modulate(rows_s, x_s, 0)

    outs_p, outs_s = [], []
    for l in range(depth):
        lam_init = 0.8 - 0.6 * math.exp(-0.3 * l)
        nxt = (l + 1, 1, 0) if l + 1 < depth else None

        def mixer_inputs(rows, h):
            tm = _tile(rows.n_rows, 1024)
            proj = _mm([h], w_in_b, l, tm, _tile(w_in.shape[2], 768), F32, 48, "in_proj")
            return _post(rows, proj, ln_c_g, ln_c_b, l, w_a, w_b, w_c, dh_a, dk_b)

        def channel_mix(rows, x, mix_parts):
            tm = _tile(rows.n_rows, 1024)
            mix = _mm(mix_parts, w_o_b, l, tm, _tile(d, 512), F32, 48, "out_proj")
            x1, h2 = _ln(rows, x, mix, ln1_g, ln1_b, l, 2, alpha, (l, 4, 3))
            pq = _mm([h2], wq_b, l, tm, 512, F32, 48, "peer_query")
            sel = _peer_select(pq, peer_subkeys, l, peer_heads, _tile(rows.n_rows, 256))
            ffn = _peer_dense(h2, u_b, v_b, sel, l, peer_heads, _tile(rows.n_rows, 512), 2, 56)
            return _ln(rows, x1, ffn, ln2_g, ln2_b, l, 5, alpha, nxt)

        qa, ka, va, qb, kb, vb, uc, vc = mixer_inputs(rows_p, h_p)
        oa = _moba_prompt(qa, ka, va, batch, seq, h_a)
        ob = _diff_prompt(qb, kb, vb, lamp, diff_gain, l, lam_init, batch, seq, h_b, dk_b)
        oc = _gmlp_prompt(uc, vc, w_s, b_st, l)
        x_p, h_p = channel_mix(rows_p, x_p, [oa, ob, oc])
        outs_p.append((ka, va, kb, vb))

        qa, ka, va, qb, kb, vb, uc, vc = mixer_inputs(rows_s, h_s)
        r3 = lambda a: a[:dec_batch].reshape(dec_batch, 1, a.shape[1])
        sel = _moba_pick(page_table, r3(qa), ck_a, l, h_a)
        oa = _moba_sample(page_table, sel.reshape(-1), r3(qa), r3(ka), r3(va), ck_a, cv_a, l, h_a)
        ob = _diff_sample(page_table, r3(qb), r3(kb), r3(vb), ck_b, cv_b, lamp, diff_gain, l, lam_init, h_b, dk_b)
        oc = _gmlp_sample(uc, vc, w_s, b_st, l)
        pad_rows = lambda a: jnp.pad(a.reshape(dec_batch, a.shape[2]), ((0, n_s - dec_batch), (0, 0)))
        x_s, h_s = channel_mix(rows_s, x_s, [pad_rows(oa), pad_rows(ob), oc])
        outs_s.append((ka[:dec_batch], va[:dec_batch], kb[:dec_batch], vb[:dec_batch], vc[:dec_batch]))

    stack = lambda rows, i: jnp.stack([r[i] for r in rows], axis=0)
    return (x_p.reshape(batch, seq, d),
            x_s[:dec_batch].reshape(dec_batch, 1, d),
            stack(outs_p, 0).reshape(depth, batch, seq, h_a, dh_a),
            stack(outs_p, 1).reshape(depth, batch, seq, h_a, dh_a),
            stack(outs_p, 2).reshape(depth, batch, seq, h_b, 2 * dk_b),
            stack(outs_p, 3).reshape(depth, batch, seq, h_b, dv_b),
            stack(outs_s, 0).reshape(depth, dec_batch, 1, h_a, dh_a),
            stack(outs_s, 1).reshape(depth, dec_batch, 1, h_a, dh_a),
            stack(outs_s, 2).reshape(depth, dec_batch, 1, h_b, 2 * dk_b),
            stack(outs_s, 3).reshape(depth, dec_batch, 1, h_b, dv_b),
            stack(outs_s, 4).reshape(depth, dec_batch, 1, w_c))
```

```python
import functools
import math

import jax
import jax.numpy as jnp
from jax import lax
from jax.experimental import pallas as pl
from jax.experimental.pallas import tpu as pltpu

F32 = jnp.float32
BF16 = jnp.bfloat16
HIGHEST = lax.Precision.HIGHEST
NEG_INF = float("-inf")

LANES = 128
MOBA_BLOCK = 256
MOBA_TOPK = 3
PEER_TOPK = 16
ROPE_THETA = 500000.0
LN_EPS = 1e-5
SAMPLE_ROWS = 128
VMEM_MB = 1024 * 1024


def _params(n_axes, vmem_mb):
    return pltpu.CompilerParams(dimension_semantics=("arbitrary",) * n_axes,
                                vmem_limit_bytes=vmem_mb * VMEM_MB)


def _tile(n, preferred):
    t = min(n, preferred) // LANES * LANES
    while n % t:
        t -= LANES
    return t


def _nt(a, b, **kw):
    return lax.dot_general(a, b, (((1,), (1,)), ((), ())), preferred_element_type=F32, **kw)


def _ada_kernel(c_ref, w_ref, b_ref, o_ref):
    a = jax.nn.silu(c_ref[...]).astype(BF16)
    o_ref[...] = jnp.dot(a, w_ref[...].astype(BF16), preferred_element_type=F32) + b_ref[...]


def _ada_mods(c_all, ada_w, ada_b):
    depth, d, n = ada_w.shape
    rows = c_all.shape[0]
    tn = 512
    return pl.pallas_call(
        _ada_kernel, grid=(depth, n // tn),
        in_specs=[pl.BlockSpec((rows, d), lambda l, j: (0, 0)),
                  pl.BlockSpec((None, d, tn), lambda l, j: (l, 0, j)),
                  pl.BlockSpec((None, 1, tn), lambda l, j: (l, 0, j))],
        out_specs=pl.BlockSpec((None, rows, tn), lambda l, j: (l, 0, j)),
        out_shape=jax.ShapeDtypeStruct((depth, rows, n), F32),
        compiler_params=_params(2, 40), name="ada_mods",
    )(c_all, ada_w, ada_b.reshape(depth, 1, n))


class _Rows:
    def __init__(self, n_rows, tm, mods, rope_tab, seq):
        self.n_rows, self.tm, self.mods, self.rope_tab, self.seq = n_rows, tm, mods, rope_tab, seq
        self.per_row = mods.ndim == 4
        self.d = mods.shape[-1]

    def mod_spec(self, layer, k, tm=None):
        tm = tm or self.tm
        if self.per_row:
            return pl.BlockSpec((None, None, tm, self.d), lambda i: (layer, k, i, 0))
        seq = self.seq
        return pl.BlockSpec((None, None, None, 1, self.d), lambda i: (layer, (i * tm) // seq, k, 0, 0))

    def rope_spec(self, tm=None):
        tm = tm or self.tm
        w = self.rope_tab.shape[1]
        if self.per_row:
            return pl.BlockSpec((tm, w), lambda i: (i, 0))
        per_seq = self.seq // tm
        return pl.BlockSpec((tm, w), lambda i: (i % per_seq, 0))


def _modulate_kernel(x_ref, sc_ref, sh_ref, h_ref):
    h_ref[...] = (x_ref[...] * (1.0 + sc_ref[...]) + sh_ref[...]).astype(BF16)


def _modulate(rows, x, layer):
    tm, d = rows.tm, rows.d
    return pl.pallas_call(
        _modulate_kernel, grid=(rows.n_rows // tm,),
        in_specs=[pl.BlockSpec((tm, d), lambda i: (i, 0)), rows.mod_spec(layer, 1), rows.mod_spec(layer, 0)],
        out_specs=pl.BlockSpec((tm, d), lambda i: (i, 0)),
        out_shape=jax.ShapeDtypeStruct((rows.n_rows, d), BF16),
        compiler_params=_params(1, 32), name="modulate",
    )(x, rows.mods, rows.mods)


def _mm_kernel(*refs, ksizes):
    b_ref, o_ref = refs[len(ksizes)], refs[-1]
    acc, off = None, 0
    for a_ref, k in zip(refs, ksizes):
        part = jnp.dot(a_ref[...], b_ref[off:off + k, :], preferred_element_type=F32)
        acc = part if acc is None else acc + part
        off += k
    o_ref[...] = acc.astype(o_ref.dtype)


def _mm(a_list, b, layer, tm, tn, out_dtype, vmem_mb, name):
    m = a_list[0].shape[0]
    ks = tuple(a.shape[1] for a in a_list)
    k_all, n = b.shape[1], b.shape[2]
    assert sum(ks) == k_all and m % tm == 0 and n % tn == 0
    in_specs = [pl.BlockSpec((tm, k), lambda i, j: (i, 0)) for k in ks]
    in_specs.append(pl.BlockSpec((None, k_all, tn), lambda i, j: (layer, 0, j)))
    return pl.pallas_call(
        functools.partial(_mm_kernel, ksizes=ks), grid=(m // tm, n // tn), in_specs=in_specs,
        out_specs=pl.BlockSpec((tm, tn), lambda i, j: (i, j)),
        out_shape=jax.ShapeDtypeStruct((m, n), out_dtype),
        compiler_params=_params(2, vmem_mb), name=name,
    )(*a_list, b)


def _rope_tables(pos, dh_a, dk_b):
    lane = jnp.arange(LANES)

    def tabs(dh):
        r, half = dh // 4, dh // 8
        inv = ROPE_THETA ** (-jnp.arange(half, dtype=F32) * (2.0 / r))
        ang = pos.astype(F32)[:, None] * inv[None, :]
        cos, sin = jnp.cos(ang), jnp.sin(ang)
        ld = lane % dh
        idx = ld % half
        c = jnp.where(ld < r, cos[:, idx], 1.0)
        s_up = jnp.where((ld >= half) & (ld < r), sin[:, idx], 0.0)
        s_dn = jnp.where(ld < half, -sin[:, idx], 0.0)
        return [c, s_up, s_dn]

    return jnp.concatenate(tabs(dh_a) + tabs(dk_b), axis=1).astype(F32)


def _post_kernel(*refs, w_a, w_b, w_c, half_a, half_b, n_prev, stacked):
    p_ref, tab_ref, lcg_ref, lcb_ref = refs[:4]
    qa_ref, ka_ref, va_ref, qb_ref, kb_ref, vb_ref, uc_ref, vc_ref = refs[4 + n_prev:12 + n_prev]
    cache_refs = refs[12 + n_prev:] if stacked else (None,) * 4
    tab = tab_ref[...]
    t = [tab[:, i * LANES:(i + 1) * LANES] for i in range(6)]

    def rope(x, c, s_up, s_dn, half):
        return x * c + pltpu.roll(x, half, 1) * s_up + pltpu.roll(x, LANES - half, 1) * s_dn

    def split_cols(src_off, dst_ref, width, tabs=None, half=None, slab_ref=None):
        for j in range(width // LANES):
            x = p_ref[:, src_off + j * LANES: src_off + (j + 1) * LANES]
            if tabs is not None:
                x = rope(x, *tabs, half)
            dst_ref[:, j * LANES:(j + 1) * LANES] = x.astype(dst_ref.dtype)
            if slab_ref is not None:
                slab_ref[:, j, :] = x

    off = 0
    split_cols(off, qa_ref, w_a, t[0:3], half_a); off += w_a
    split_cols(off, ka_ref, w_a, t[0:3], half_a, cache_refs[0]); off += w_a
    split_cols(off, va_ref, w_a, slab_ref=cache_refs[1]); off += w_a
    split_cols(off, qb_ref, w_b, t[3:6], half_b); off += w_b
    split_cols(off, kb_ref, w_b, t[3:6], half_b, cache_refs[2]); off += w_b
    split_cols(off, vb_ref, w_b, slab_ref=cache_refs[3]); off += w_b
    uc_ref[...] = jax.nn.gelu(p_ref[:, off:off + w_c]); off += w_c
    g = jax.nn.gelu(p_ref[:, off:off + w_c])
    mu = jnp.mean(g, axis=-1, keepdims=True)
    gc = g - mu
    var = jnp.mean(gc * gc, axis=-1, keepdims=True)
    vc_ref[...] = gc * lax.rsqrt(var + LN_EPS) * lcg_ref[...] + lcb_ref[...]


def _post(rows, proj, ln_c_g, ln_c_b, layer, w_a, w_b, w_c, dh_a, dk_b, stacked=False, prev=()):
    tm, n = rows.tm, rows.n_rows
    depth = ln_c_g.shape[0]
    widths = (w_a, w_a, w_a, w_b, w_b, w_b, w_c, w_c)
    kv = BF16 if stacked else F32
    dtypes = (BF16, kv, kv, BF16, kv, kv, F32, F32)
    row_spec = lambda w: pl.BlockSpec((tm, w), lambda i: (i, 0))
    in_specs = [row_spec(proj.shape[1]), rows.rope_spec(),
                pl.BlockSpec((None, 1, w_c), lambda i: (layer, 0, 0)),
                pl.BlockSpec((None, 1, w_c), lambda i: (layer, 0, 0))]
    out_specs = [row_spec(w) for w in widths]
    out_shape = [jax.ShapeDtypeStruct((n, w), dt) for w, dt in zip(widths, dtypes)]
    aliases = {}
    if stacked:
        for k, w in enumerate((w_a, w_a, w_b, w_b)):
            heads = w // LANES
            out_specs.append(pl.BlockSpec((None, tm, heads, LANES), lambda i: (layer, i, 0, 0)))
            out_shape.append(jax.ShapeDtypeStruct((depth, n, heads, LANES), F32))
        in_specs += [pl.BlockSpec(memory_space=pl.ANY)] * len(prev)
        aliases = {4 + k: 8 + k for k in range(len(prev))}
    return pl.pallas_call(
        functools.partial(_post_kernel, w_a=w_a, w_b=w_b, w_c=w_c, half_a=dh_a // 8, half_b=dk_b // 8,
                          n_prev=len(prev), stacked=stacked),
        grid=(n // tm,), in_specs=in_specs, out_specs=out_specs, out_shape=out_shape,
        input_output_aliases=aliases,
        compiler_params=_params(1, 48), name="proj_split",
    )(proj, rows.rope_tab, ln_c_g.reshape(depth, 1, w_c), ln_c_b.reshape(depth, 1, w_c), *prev)


def _moba_kernel(q_ref, k_ref, v_ref, o_ref, km_ref, *, nb, scale):
    blk = MOBA_BLOCK
    qi = pl.program_id(2)

    @pl.when(qi == 0)
    def _():
        km_ref[...] = jnp.zeros_like(km_ref)
        for j in range(nb):
            km_ref[j:j + 1, :] = jnp.mean(k_ref[j * blk:(j + 1) * blk, :].astype(F32), axis=0, keepdims=True)

    q = q_ref[...]
    gate = _nt(q.astype(F32), km_ref[...], precision=HIGHEST)
    lane = lax.broadcasted_iota(jnp.int32, (blk, LANES), 1)
    g = jnp.where(lane < qi, gate, NEG_INF)
    rank = jnp.zeros((blk, LANES), F32)
    for m in range(nb):
        gm = g[:, m:m + 1]
        tie_first = jnp.where(lane > m, 1.0, 0.0)
        rank = rank + jnp.where(gm > g, 1.0, jnp.where(gm == g, tie_first, 0.0))
    sel = jnp.where(lane < qi, jnp.where(rank < MOBA_TOPK, 1.0, 0.0), 0.0)

    def block(j):
        start = pl.multiple_of(j * blk, blk)
        kj = k_ref[pl.ds(start, blk), :].astype(BF16)
        vj = v_ref[pl.ds(start, blk), :].astype(BF16)
        return _nt(q, kj) * scale, vj

    s, vj = block(qi)
    row = lax.broadcasted_iota(jnp.int32, (blk, blk), 0)
    col = lax.broadcasted_iota(jnp.int32, (blk, blk), 1)
    s = jnp.where(col <= row, s, NEG_INF)
    m0 = jnp.max(s, axis=1, keepdims=True)
    p = jnp.exp(s - m0)
    l0 = jnp.sum(p, axis=1, keepdims=True)
    acc0 = jnp.dot(p.astype(BF16), vj, preferred_element_type=F32)

    def body(j, carry):
        m_run, l_run, acc = carry
        s, vj = block(j)
        chosen = jnp.max(jnp.where(lane == j, sel, 0.0), axis=1, keepdims=True)
        s = jnp.where(chosen > 0.0, s, NEG_INF)
        m_new = jnp.maximum(m_run, jnp.max(s, axis=1, keepdims=True))
        alpha = jnp.exp(m_run - m_new)
        p = jnp.exp(s - m_new)
        l_new = alpha * l_run + jnp.sum(p, axis=1, keepdims=True)
        acc = alpha * acc + jnp.dot(p.astype(BF16), vj, preferred_element_type=F32)
        return m_new, l_new, acc

    _, l_fin, acc = lax.fori_loop(0, qi, body, (m0, l0, acc0))
    o_ref[...] = (acc / l_fin).astype(o_ref.dtype)


def _moba_prompt(q, k, v, batch, seq, heads):
    nb = seq // MOBA_BLOCK
    assert seq % MOBA_BLOCK == 0 and nb <= LANES
    return pl.pallas_call(
        functools.partial(_moba_kernel, nb=nb, scale=LANES ** -0.5),
        grid=(batch, heads, nb),
        in_specs=[pl.BlockSpec((MOBA_BLOCK, LANES), lambda b, h, i: (b * nb + i, h)),
                  pl.BlockSpec((seq, LANES), lambda b, h, i: (b, h)),
                  pl.BlockSpec((seq, LANES), lambda b, h, i: (b, h))],
        out_specs=pl.BlockSpec((MOBA_BLOCK, LANES), lambda b, h, i: (b * nb + i, h)),
        out_shape=jax.ShapeDtypeStruct(q.shape, BF16),
        scratch_shapes=[pltpu.VMEM((LANES, LANES), F32)],
        compiler_params=_params(3, 32), name="moba_prompt",
    )(q, k, v)


def _lam(lamp_ref, lam_init):
    lp = lamp_ref[...]
    d1 = jnp.sum(lp[0:1] * lp[1:2], axis=1, keepdims=True)
    d2 = jnp.sum(lp[2:3] * lp[3:4], axis=1, keepdims=True)
    return jnp.exp(d1) - jnp.exp(d2) + lam_init


def _diff_finish(o, gain, lam_init):
    ms = jnp.mean(o * o, axis=-1, keepdims=True)
    return o * lax.rsqrt(ms + LN_EPS) * gain * (1.0 - lam_init)


def _diff_kernel(lamp_ref, gain_ref, q_ref, k_ref, v_ref, o_ref, *, tq, dk, lam_init):
    qi = pl.program_id(2)
    scale = dk ** -0.5
    lam = _lam(lamp_ref, lam_init)
    q = q_ref[...]
    lane = lax.broadcasted_iota(jnp.int32, (tq, LANES), 1)
    zero = jnp.zeros_like(q)
    q1 = jnp.where(lane < dk, q, zero)
    q2 = jnp.where(lane >= dk, q, zero)

    def scores(j):
        start = pl.multiple_of(j * tq, tq)
        kj = k_ref[pl.ds(start, tq), :].astype(BF16)
        vj = v_ref[pl.ds(start, tq), :].astype(BF16)
        return _nt(q1, kj) * scale, _nt(q2, kj) * scale, vj

    def first(s, vj):
        m = jnp.max(s, axis=1, keepdims=True)
        p = jnp.exp(s - m)
        return m, jnp.sum(p, axis=1, keepdims=True), jnp.dot(p.astype(BF16), vj, preferred_element_type=F32)

    def update(state, s, vj):
        m_run, l_run, acc = state
        m_new = jnp.maximum(m_run, jnp.max(s, axis=1, keepdims=True))
        alpha = jnp.exp(m_run - m_new)
        p = jnp.exp(s - m_new)
        return (m_new, alpha * l_run + jnp.sum(p, axis=1, keepdims=True),
                alpha * acc + jnp.dot(p.astype(BF16), vj, preferred_element_type=F32))

    s1, s2, vj = scores(qi)
    row = lax.broadcasted_iota(jnp.int32, (tq, tq), 0)
    col = lax.broadcasted_iota(jnp.int32, (tq, tq), 1)
    causal = col <= row
    st1 = first(jnp.where(causal, s1, NEG_INF), vj)
    st2 = first(jnp.where(causal, s2, NEG_INF), vj)

    def body(j, carry):
        a, b = carry
        s1, s2, vj = scores(j)
        return update(a, s1, vj), update(b, s2, vj)

    st1, st2 = lax.fori_loop(0, qi, body, (st1, st2))
    o = st1[2] / st1[1] - lam * (st2[2] / st2[1])
    o_ref[...] = _diff_finish(o, gain_ref[...], lam_init).astype(o_ref.dtype)


def _diff_prompt(q, k, v, lamp, gain, layer, lam_init, batch, seq, heads, dk):
    tq = 256
    nq = seq // tq
    depth = lamp.shape[0]
    return pl.pallas_call(
        functools.partial(_diff_kernel, tq=tq, dk=dk, lam_init=lam_init),
        grid=(batch, heads, nq),
        in_specs=[pl.BlockSpec((None, 4, dk), lambda b, h, i: (layer, 0, 0)),
                  pl.BlockSpec((None, 1, LANES), lambda b, h, i: (layer, 0, 0)),
                  pl.BlockSpec((tq, LANES), lambda b, h, i: (b * nq + i, h)),
                  pl.BlockSpec((seq, LANES), lambda b, h, i: (b, h)),
                  pl.BlockSpec((seq, LANES), lambda b, h, i: (b, h))],
        out_specs=pl.BlockSpec((tq, LANES), lambda b, h, i: (b * nq + i, h)),
        out_shape=jax.ShapeDtypeStruct(q.shape, BF16),
        compiler_params=_params(3, 32), name="diff_prompt",
    )(lamp, gain.reshape(depth, 1, LANES), q, k, v)


def _gmlp_kernel(u_ref, v_ref, w_ref, bt_ref, o_ref, *, groups, chunk):
    row = lax.broadcasted_iota(jnp.int32, (chunk, chunk), 0)
    col = lax.broadcasted_iota(jnp.int32, (chunk, chunk), 1)
    causal = col <= row
    for g in range(groups):
        sl = slice(g * LANES, (g + 1) * LANES)
        w = jnp.where(causal, w_ref[g], 0.0).astype(BF16)
        mixed = jnp.dot(w, v_ref[:, sl].astype(BF16), preferred_element_type=F32) + bt_ref[:, g:g + 1]
        o_ref[:, sl] = (u_ref[:, sl] * mixed).astype(o_ref.dtype)


def _gmlp_prompt(u, v, w_s, b_st, layer):
    n, w_c = u.shape
    groups, chunk = w_s.shape[1], w_s.shape[2]
    return pl.pallas_call(
        functools.partial(_gmlp_kernel, groups=groups, chunk=chunk), grid=(n // chunk,),
        in_specs=[pl.BlockSpec((chunk, w_c), lambda i: (i, 0)), pl.BlockSpec((chunk, w_c), lambda i: (i, 0)),
                  pl.BlockSpec((None, groups, chunk, chunk), lambda i: (layer, 0, 0, 0)),
                  pl.BlockSpec((None, chunk, groups), lambda i: (layer, 0, 0))],
        out_specs=pl.BlockSpec((chunk, w_c), lambda i: (i, 0)),
        out_shape=jax.ShapeDtypeStruct((n, w_c), BF16),
        compiler_params=_params(1, 32), name="gmlp_prompt",
    )(u, v, w_s, b_st)


def _gmlp_first_kernel(u_ref, v_ref, w_ref, bt_ref, o_ref, *, groups):
    for g in range(groups):
        sl = slice(g * LANES, (g + 1) * LANES)
        mixed = w_ref[g, 0:1, 0:1] * v_ref[:, sl] + bt_ref[0:1, g:g + 1]
        o_ref[:, sl] = (u_ref[:, sl] * mixed).astype(o_ref.dtype)


def _gmlp_sample(u, v, w_s, b_st, layer):
    n, w_c = u.shape
    groups, chunk = w_s.shape[1], w_s.shape[2]
    return pl.pallas_call(
        functools.partial(_gmlp_first_kernel, groups=groups), grid=(1,),
        in_specs=[pl.BlockSpec((n, w_c), lambda i: (0, 0)), pl.BlockSpec((n, w_c), lambda i: (0, 0)),
                  pl.BlockSpec((None, groups, chunk, chunk), lambda i: (layer, 0, 0, 0)),
                  pl.BlockSpec((None, chunk, groups), lambda i: (layer, 0, 0))],
        out_specs=pl.BlockSpec((n, w_c), lambda i: (0, 0)),
        out_shape=jax.ShapeDtypeStruct((n, w_c), BF16),
        compiler_params=_params(1, 32), name="gmlp_sample",
    )(u, v, w_s, b_st)


def _ln_kernel(*refs, alpha, emit_h):
    x_ref, y_ref, gate_ref, g_ref, b_ref = refs[:5]
    z = alpha * x_ref[...] + gate_ref[...] * y_ref[...]
    mu = jnp.mean(z, axis=-1, keepdims=True)
    zc = z - mu
    var = jnp.mean(zc * zc, axis=-1, keepdims=True)
    xn = zc * lax.rsqrt(var + LN_EPS) * g_ref[...] + b_ref[...]
    if emit_h:
        sc_ref, sh_ref, xo_ref, h_ref = refs[5:]
        h_ref[...] = (xn * (1.0 + sc_ref[...]) + sh_ref[...]).astype(BF16)
    else:
        xo_ref = refs[5]
    xo_ref[...] = xn


def _ln(rows, x, y, ln_g, ln_b, layer, gate_k, alpha, next_mod=None):
    tm, d, n = min(rows.tm, 256), rows.d, rows.n_rows
    depth = ln_g.shape[0]
    row_spec = pl.BlockSpec((tm, d), lambda i: (i, 0))
    par_spec = pl.BlockSpec((None, 1, d), lambda i: (layer, 0, 0))
    in_specs = [row_spec, row_spec, rows.mod_spec(layer, gate_k, tm), par_spec, par_spec]
    args = [x, y, rows.mods, ln_g.reshape(depth, 1, d), ln_b.reshape(depth, 1, d)]
    out_specs, out_shape = [row_spec], [jax.ShapeDtypeStruct((n, d), F32)]
    if next_mod is not None:
        nl, k_sc, k_sh = next_mod
        in_specs += [rows.mod_spec(nl, k_sc, tm), rows.mod_spec(nl, k_sh, tm)]
        args += [rows.mods, rows.mods]
        out_specs.append(row_spec)
        out_shape.append(jax.ShapeDtypeStruct((n, d), BF16))
    out = pl.pallas_call(
        functools.partial(_ln_kernel, alpha=alpha, emit_h=next_mod is not None), grid=(n // tm,),
        in_specs=in_specs, out_specs=out_specs, out_shape=out_shape,
        compiler_params=_params(1, 48), name="deepnorm_ln",
    )(*args)
    return (out[0], out[1]) if next_mod is not None else (out[0], None)


def _peer_select_kernel(q_ref, keys_ref, s1_ref, e1_ref, s2_ref, e2_ref, tau_ref, cand_ref, *, heads, n_keys, pairs):
    tm = q_ref.shape[0]
    sub = lax.broadcasted_iota(jnp.int32, (n_keys, tm), 0)
    csub = lax.broadcasted_iota(jnp.int32, cand_ref.shape, 0)

    def pop_max(x, iota, big):
        m = jnp.max(x, axis=0, keepdims=True)
        first = jnp.min(jnp.where(x == m, iota, big), axis=0, keepdims=True)
        return m, jnp.where(iota == first, NEG_INF, x)

    def top_rows(s):
        rows, x = [], s
        for _ in range(PEER_TOPK):
            m, x = pop_max(x, sub, n_keys)
            rows.append(m)
        return rows

    for h in range(heads):
        s1 = _nt(keys_ref[0], q_ref[:, (2 * h) * n_keys:(2 * h + 1) * n_keys], precision=HIGHEST)
        s2 = _nt(keys_ref[1], q_ref[:, (2 * h + 1) * n_keys:(2 * h + 2) * n_keys], precision=HIGHEST)
        a1, a2 = top_rows(s1), top_rows(s2)
        cand_ref[...] = jnp.full(cand_ref.shape, NEG_INF, F32)
        for r, (i, j) in enumerate(pairs):
            cand_ref[r:r + 1, :] = a1[i] + a2[j]
        c = cand_ref[...]
        x, tau = c, None
        for _ in range(PEER_TOPK):
            tau, x = pop_max(x, csub, cand_ref.shape[0])
        cmax = a1[0] + a2[0]
        z = jnp.sum(jnp.where(c >= tau, jnp.exp(c - cmax), 0.0), axis=0, keepdims=True)
        s1_ref[h] = s1
        s2_ref[h] = s2
        e1_ref[h] = jnp.exp(s1 - a1[0]) / z
        e2_ref[h] = jnp.exp(s2 - a2[0])
        tau_ref[h] = tau


def _peer_select(q, keys, layer, heads, tm):
    n, n_keys = q.shape[0], keys.shape[2]
    pairs = tuple((i, j) for i in range(PEER_TOPK) for j in range(PEER_TOPK) if (i + 1) * (j + 1) <= PEER_TOPK)
    n_cand = -(-len(pairs) // 8) * 8
    big = pl.BlockSpec((heads, n_keys, tm), lambda i: (0, 0, i))
    big_shape = jax.ShapeDtypeStruct((heads, n_keys, n), F32)
    return pl.pallas_call(
        functools.partial(_peer_select_kernel, heads=heads, n_keys=n_keys, pairs=pairs), grid=(n // tm,),
        in_specs=[pl.BlockSpec((tm, q.shape[1]), lambda i: (i, 0)),
                  pl.BlockSpec((None, 2, n_keys, keys.shape[3]), lambda i: (layer, 0, 0, 0))],
        out_specs=[big, big, big, big, pl.BlockSpec((heads, 1, tm), lambda i: (0, 0, i))],
        out_shape=[big_shape] * 4 + [jax.ShapeDtypeStruct((heads, 1, n), F32)],
        scratch_shapes=[pltpu.VMEM((n_cand, tm), F32)],
        compiler_params=_params(1, 32), name="peer_select",
    )(q, keys)


def _peer_dense_kernel(ht_ref, u_ref, vt_ref, s1_ref, e1_ref, s2_ref, e2_ref, tau_ref, o_ref, w_ref, st_ref,
                       *, heads, n_keys, n1_per_part, parts):
    j = pl.program_id(1)
    n1_per_step = n1_per_part * parts
    tm = ht_ref.shape[1]
    @pl.when(j == 0)
    def _():
        o_ref[...] = jnp.zeros_like(o_ref)

    s1_rows = [[s1_ref[h, pl.ds(j * n1_per_step + r, 1), :] for r in range(n1_per_step)] for h in range(heads)]
    e1_rows = [[e1_ref[h, pl.ds(j * n1_per_step + r, 1), :] for r in range(n1_per_step)] for h in range(heads)]
    for r0 in range(0, n1_per_step, 2):
        pair = slice(r0 * n_keys, (r0 + 2) * n_keys)
        for c in range(tm // LANES):
            cols = slice(c * LANES, (c + 1) * LANES)
            g = [None, None]
            for h in range(heads):
                s2, e2, tau = s2_ref[h, :, cols], e2_ref[h, :, cols], tau_ref[h, :, cols]
                for k in range(2):
                    s1 = s1_rows[h][r0 + k][:, cols]
                    e1 = e1_rows[h][r0 + k][:, cols]
                    term = e1 * jnp.where(s1 + s2 >= tau, e2, 0.0)
                    g[k] = term if g[k] is None else g[k] + term
            for k in range(2):
                st_ref[(r0 + k) * n_keys:(r0 + k + 1) * n_keys, cols] = g[k]
        st = jnp.dot(u_ref[pair, :], ht_ref[...], preferred_element_type=F32)
        w_ref[pair, :] = (st_ref[pair, :] * jax.nn.gelu(st)).astype(BF16)
    rows_per_chunk = 512
    for m in range(vt_ref.shape[0] // rows_per_chunk):
        rows = slice(m * rows_per_chunk, (m + 1) * rows_per_chunk)
        part = jnp.dot(vt_ref[rows, :], w_ref[...], preferred_element_type=F32)
        o_ref[rows, :] += part


def _peer_dense(h2t, u, vt, sel, layer, heads, tm, n1_per_part, parts, vmem_mb):
    d, n = h2t.shape
    n_keys = sel[0].shape[1]
    et = parts * n1_per_part * n_keys
    n_steps = n_keys * n_keys // et
    once = pl.Buffered(1)
    resident = pl.BlockSpec((heads, n_keys, tm), lambda i, j: (0, 0, i), pipeline_mode=once)
    return pl.pallas_call(
        functools.partial(_peer_dense_kernel, heads=heads, n_keys=n_keys, n1_per_part=n1_per_part, parts=parts),
        grid=(n // tm, n_steps),
        in_specs=[pl.BlockSpec((d, tm), lambda i, j: (0, i), pipeline_mode=once),
                  pl.BlockSpec((None, et, d), lambda i, j: (layer, j, 0)),
                  pl.BlockSpec((None, d, et), lambda i, j: (layer, 0, j)),
                  resident, resident, resident, resident,
                  pl.BlockSpec((heads, 1, tm), lambda i, j: (0, 0, i))],
        out_specs=pl.BlockSpec((d, tm), lambda i, j: (0, i)),
        out_shape=jax.ShapeDtypeStruct((d, n), F32),
        scratch_shapes=[pltpu.VMEM((et, tm), BF16), pltpu.VMEM((et, tm), F32)],
        compiler_params=_params(2, vmem_mb), name="peer_dense",
    )(h2t, u, vt, *sel)


def _moba_decode_kernel(pt_ref, q_ref, kn_ref, vn_ref, k_ref, v_ref, o_ref, ks_ref, m_ref, l_ref, acc_ref,
                        *, pages_per_block, n_pages, scale):
    del pt_ref
    p = pl.program_id(1)
    blk = p // pages_per_block
    qf = q_ref[...].astype(F32)
    k3, v3 = k_ref[...], v_ref[...]
    s = jnp.broadcast_to(jnp.sum(k3 * qf, axis=-1, keepdims=True) * scale, k3.shape)
    m_blk = jnp.max(s, axis=0)
    ksum = jnp.sum(k3, axis=0)

    @pl.when(p % pages_per_block == 0)
    def _():
        pe = jnp.exp(s - m_blk)
        m_ref[blk] = m_blk
        l_ref[blk] = jnp.sum(pe, axis=0)
        acc_ref[blk] = jnp.sum(pe * v3, axis=0)
        ks_ref[blk] = ksum

    @pl.when(p % pages_per_block != 0)
    def _():
        m_old = m_ref[blk]
        m_new = jnp.maximum(m_old, m_blk)
        alpha = jnp.exp(m_old - m_new)
        pe = jnp.exp(s - m_new)
        m_ref[blk] = m_new
        l_ref[blk] = alpha * l_ref[blk] + jnp.sum(pe, axis=0)
        acc_ref[blk] = alpha * acc_ref[blk] + jnp.sum(pe * v3, axis=0)
        ks_ref[blk] = ks_ref[blk] + ksum

    @pl.when(p == n_pages - 1)
    def _():
        shape = ks_ref.shape
        nb = shape[0]
        gate = jnp.sum(ks_ref[...] * (1.0 / MOBA_BLOCK) * qf, axis=-1, keepdims=True)
        g = jnp.broadcast_to(gate, shape)
        bi = lax.broadcasted_iota(jnp.int32, shape, 0)
        chosen = jnp.zeros(shape, F32)
        for _ in range(MOBA_TOPK):
            gmax = jnp.max(g, axis=0)
            first = jnp.min(jnp.where(g == gmax, bi, nb), axis=0)
            hit = bi == first
            chosen = jnp.where(hit, 1.0, chosen)
            g = jnp.where(hit, NEG_INF, g)
        s_own = jnp.broadcast_to(jnp.sum(qf * kn_ref[...], axis=-1, keepdims=True) * scale, qf.shape)
        m_all = m_ref[...]
        m_fin = jnp.maximum(s_own, jnp.max(jnp.where(chosen > 0.0, m_all, NEG_INF), axis=0))
        wts = jnp.where(chosen > 0.0, jnp.exp(m_all - m_fin), 0.0)
        w_own = jnp.exp(s_own - m_fin)
        l_fin = jnp.sum(wts * l_ref[...], axis=0) + w_own
        o = jnp.sum(wts * acc_ref[...], axis=0) + w_own * vn_ref[...]
        o_ref[...] = (o / l_fin).astype(o_ref.dtype)


def _moba_decode(page_table, q3, kn3, vn3, cache_k, cache_v, layer):
    batch, n_pages = page_table.shape
    page, heads = cache_k.shape[2], cache_k.shape[3]
    ppb = MOBA_BLOCK // page
    nb = n_pages // ppb
    assert n_pages % ppb == 0 and nb >= MOBA_TOPK
    row = pl.BlockSpec((None, heads, LANES), lambda b, p, pt: (b, 0, 0))
    page_spec = pl.BlockSpec((None, None, page, heads, LANES), lambda b, p, pt: (layer, pt[b, p], 0, 0, 0))
    return pl.pallas_call(
        functools.partial(_moba_decode_kernel, pages_per_block=ppb, n_pages=n_pages, scale=LANES ** -0.5),
        grid_spec=pltpu.PrefetchScalarGridSpec(
            num_scalar_prefetch=1, grid=(batch, n_pages),
            in_specs=[row, row, row, page_spec, page_spec],
            out_specs=row,
            scratch_shapes=[pltpu.VMEM((nb, heads, LANES), F32)] * 4),
        out_shape=jax.ShapeDtypeStruct((batch, heads, LANES), BF16),
        compiler_params=_params(2, 32), name="moba_decode",
    )(page_table, q3, kn3, vn3, cache_k, cache_v)


def _diff_decode_kernel(pt_ref, lamp_ref, gain_ref, q_ref, kn_ref, vn_ref, k_ref, v_ref, o_ref, m_ref, l_ref, acc_ref,
                        *, dk, n_pages, lam_init):
    del pt_ref
    p = pl.program_id(1)
    scale = dk ** -0.5
    qf = q_ref[...].astype(F32)
    heads = qf.shape[0]
    spread = jnp.where(lax.broadcasted_iota(jnp.int32, (LANES, 2 * LANES), 0) // dk
                       == lax.broadcasted_iota(jnp.int32, (LANES, 2 * LANES), 1) // LANES, 1.0, 0.0).astype(BF16)

    def map_scores(prod):
        hi = prod.astype(BF16)
        lo = (prod - hi.astype(F32)).astype(BF16)
        s = (jnp.dot(hi, spread, preferred_element_type=F32) + jnp.dot(lo, spread, preferred_element_type=F32)) * scale
        return s[:, :LANES], s[:, LANES:]

    @pl.when(p == 0)
    def _():
        s_new = map_scores(kn_ref[...] * qf)
        for i in range(2):
            m_ref[i] = s_new[i]
            l_ref[i] = jnp.ones_like(qf)
            acc_ref[i] = vn_ref[...]

    k3, v3 = k_ref[...], v_ref[...]
    page = k3.shape[0]
    scores = map_scores((k3 * qf).reshape(page * heads, LANES))
    for i in range(2):
        s = scores[i].reshape(page, heads, LANES)
        m_old = m_ref[i]
        m_new = jnp.maximum(m_old, jnp.max(s, axis=0))
        alpha = jnp.exp(m_old - m_new)
        pe = jnp.exp(s - m_new)
        m_ref[i] = m_new
        l_ref[i] = alpha * l_ref[i] + jnp.sum(pe, axis=0)
        acc_ref[i] = alpha * acc_ref[i] + jnp.sum(pe * v3, axis=0)

    @pl.when(p == n_pages - 1)
    def _():
        lam = _lam(lamp_ref, lam_init)
        o = acc_ref[0] / l_ref[0] - lam * (acc_ref[1] / l_ref[1])
        o_ref[...] = _diff_finish(o, gain_ref[...], lam_init).astype(o_ref.dtype)


def _diff_decode(page_table, q3, kn3, vn3, cache_k, cache_v, lamp, gain, layer, lam_init, dk):
    batch, n_pages = page_table.shape
    page, heads = cache_k.shape[2], cache_k.shape[3]
    depth = lamp.shape[0]
    row = pl.BlockSpec((None, heads, LANES), lambda b, p, pt: (b, 0, 0))
    page_spec = pl.BlockSpec((None, None, page, heads, LANES), lambda b, p, pt: (layer, pt[b, p], 0, 0, 0))
    return pl.pallas_call(
        functools.partial(_diff_decode_kernel, dk=dk, n_pages=n_pages, lam_init=lam_init),
        grid_spec=pltpu.PrefetchScalarGridSpec(
            num_scalar_prefetch=1, grid=(batch, n_pages),
            in_specs=[pl.BlockSpec((None, 4, dk), lambda b, p, pt: (layer, 0, 0)),
                      pl.BlockSpec((None, 1, LANES), lambda b, p, pt: (layer, 0, 0)),
                      row, row, row, page_spec, page_spec],
            out_specs=row,
            scratch_shapes=[pltpu.VMEM((2, heads, LANES), F32)] * 3),
        out_shape=jax.ShapeDtypeStruct((batch, heads, LANES), BF16),
        compiler_params=_params(2, 32), name="diff_decode",
    )(page_table, lamp, gain.reshape(depth, 1, LANES), q3, kn3, vn3, cache_k, cache_v)


def kernel(x_prompt, x_sample, cache_a_k, cache_a_v, cache_b_k, cache_b_v, page_table, c_prompt, c_sample, ada_w, ada_b, w_in, w_o, lam_q1, lam_k1, lam_q2, lam_k2, diff_gain, ln_c_g, ln_c_b, w_s, b_s, ln1_g, ln1_b, ln2_g, ln2_b, peer_wq, peer_subkeys, peer_u, peer_v):
    batch, seq, d = x_prompt.shape
    dec_batch, dec_seq, _ = x_sample.shape
    depth = w_in.shape[0]
    n_phys, page, h_a, dh_a = cache_a_k.shape[1:]
    h_b, dv_b = cache_b_v.shape[3:]
    dk_b = cache_b_k.shape[4] // 2
    w_a, w_b, w_c = h_a * dh_a, h_b * dv_b, ln_c_g.shape[1]
    n_keys = peer_subkeys.shape[2]
    peer_heads = peer_wq.shape[2] // (2 * peer_subkeys.shape[3])
    past_len = page_table.shape[1] * page
    assert dec_seq == 1 and dh_a == LANES and dv_b == LANES and n_keys == LANES and peer_subkeys.shape[3] == LANES
    assert past_len % MOBA_BLOCK == 0 and w_s.shape[2] == LANES and dec_batch <= SAMPLE_ROWS
    alpha = (2 * depth) ** 0.25
    n_p, n_s = batch * seq, SAMPLE_ROWS

    w_in_b, w_o_b, wq_b = w_in.astype(BF16), w_o.astype(BF16), peer_wq.astype(BF16)
    u_b, vt_b = peer_u.astype(BF16), jnp.swapaxes(peer_v, 1, 2).astype(BF16)
    lamp = jnp.stack([lam_q1, lam_k1, lam_q2, lam_k2], axis=1)
    b_st = jnp.swapaxes(b_s, 1, 2)

    c_rows = -(-(batch + dec_batch) // 16) * 16
    c_all = jnp.zeros((c_rows, d), F32).at[:batch].set(c_prompt).at[batch:batch + dec_batch].set(c_sample)
    mods = _ada_mods(c_all, ada_w, ada_b)
    mods_p = mods[:, :batch].reshape(depth, batch, 6, 1, d)
    mods_s = mods[:, batch:batch + dec_batch].reshape(depth, dec_batch, 6, d).transpose(0, 2, 1, 3)
    mods_s = jnp.pad(mods_s, ((0, 0), (0, 0), (0, n_s - dec_batch), (0, 0)))

    rows_p = _Rows(n_p, 128, mods_p, _rope_tables(jnp.arange(seq), dh_a, dk_b), seq)
    rows_s = _Rows(n_s, n_s, mods_s, _rope_tables(jnp.full((n_s,), past_len), dh_a, dk_b), 1)

    x_p = x_prompt.reshape(n_p, d)
    x_s = jnp.pad(x_sample.reshape(dec_batch, d), ((0, n_s - dec_batch), (0, 0)))
    h_p = _modulate(rows_p, x_p, 0)
    h_s = _modulate(rows_s, x_s, 0)

    caches_p, outs_s = (), []
    for l in range(depth):
        lam_init = 0.8 - 0.6 * math.exp(-0.3 * l)
        nxt = (l + 1, 1, 0) if l + 1 < depth else None

        def mixer_inputs(rows, h, **cache_args):
            tm = _tile(rows.n_rows, 1024)
            proj = _mm([h], w_in_b, l, tm, _tile(w_in.shape[2], 768), F32, 48, "in_proj")
            return _post(rows, proj, ln_c_g, ln_c_b, l, w_a, w_b, w_c, dh_a, dk_b, **cache_args)

        def channel_mix(rows, x, mix_parts):
            tm = _tile(rows.n_rows, 1024)
            mix = _mm(mix_parts, w_o_b, l, tm, _tile(d, 512), F32, 48, "out_proj")
            x1, h2 = _ln(rows, x, mix, ln1_g, ln1_b, l, 2, alpha, (l, 4, 3))
            pq = _mm([h2], wq_b, l, tm, 512, F32, 48, "peer_query")
            sel = _peer_select(pq, peer_subkeys, l, peer_heads, _tile(rows.n_rows, 256))
            ffn_t = _peer_dense(h2.T, u_b, vt_b, sel, l, peer_heads, _tile(rows.n_rows, 512), 2, 2, 56)
            return _ln(rows, x1, ffn_t.T, ln2_g, ln2_b, l, 5, alpha, nxt)

        qa, ka, va, qb, kb, vb, uc, vc, *caches_p = mixer_inputs(rows_p, h_p, stacked=True, prev=tuple(caches_p))
        oa = _moba_prompt(qa, ka, va, batch, seq, h_a)
        ob = _diff_prompt(qb, kb, vb, lamp, diff_gain, l, lam_init, batch, seq, h_b, dk_b)
        oc = _gmlp_prompt(uc, vc, w_s, b_st, l)
        x_p, h_p = channel_mix(rows_p, x_p, [oa, ob, oc])

        qa, ka, va, qb, kb, vb, uc, vc = mixer_inputs(rows_s, h_s)
        r3 = lambda a, heads: a[:dec_batch].reshape(dec_batch, heads, LANES)
        oa = _moba_decode(page_table, r3(qa, h_a), r3(ka, h_a), r3(va, h_a), cache_a_k, cache_a_v, l)
        ob = _diff_decode(page_table, r3(qb, h_b), r3(kb, h_b), r3(vb, h_b), cache_b_k, cache_b_v, lamp, diff_gain,
                          l, lam_init, dk_b)
        oc = _gmlp_sample(uc, vc, w_s, b_st, l)
        pad_rows = lambda a: jnp.pad(a.reshape(dec_batch, -1), ((0, n_s - dec_batch), (0, 0)))
        x_s, h_s = channel_mix(rows_s, x_s, [pad_rows(oa), pad_rows(ob), oc])
        outs_s.append((ka[:dec_batch], va[:dec_batch], kb[:dec_batch], vb[:dec_batch], vc[:dec_batch]))

    stack = lambda rows, i: jnp.stack([r[i] for r in rows], axis=0)
    return (x_p.reshape(batch, seq, d),
            x_s[:dec_batch].reshape(dec_batch, 1, d),
            caches_p[0].reshape(depth, batch, seq, h_a, dh_a),
            caches_p[1].reshape(depth, batch, seq, h_a, dh_a),
            caches_p[2].reshape(depth, batch, seq, h_b, 2 * dk_b),
            caches_p[3].reshape(depth, batch, seq, h_b, dv_b),
            stack(outs_s, 0).reshape(depth, dec_batch, 1, h_a, dh_a),
            stack(outs_s, 1).reshape(depth, dec_batch, 1, h_a, dh_a),
            stack(outs_s, 2).reshape(depth, dec_batch, 1, h_b, 2 * dk_b),
            stack(outs_s, 3).reshape(depth, dec_batch, 1, h_b, dv_b),
            stack(outs_s, 4).reshape(depth, dec_batch, 1, w_c))
```

```python
import functools
import math

import jax
import jax.numpy as jnp
from jax import lax
from jax.experimental import pallas as pl
from jax.experimental.pallas import tpu as pltpu

F32 = jnp.float32
BF16 = jnp.bfloat16
HIGHEST = lax.Precision.HIGHEST
NEG_INF = float("-inf")

LANES = 128
MOBA_BLOCK = 256
MOBA_TOPK = 3
PEER_TOPK = 16
ROPE_THETA = 500000.0
LN_EPS = 1e-5
SAMPLE_ROWS = 128
VMEM_MB = 1024 * 1024


def _params(n_axes, vmem_mb):
    return pltpu.CompilerParams(dimension_semantics=("arbitrary",) * n_axes,
                                vmem_limit_bytes=vmem_mb * VMEM_MB)


def _tile(n, preferred):
    t = min(n, preferred) // LANES * LANES
    while n % t:
        t -= LANES
    return t


def _nt(a, b, **kw):
    return lax.dot_general(a, b, (((1,), (1,)), ((), ())), preferred_element_type=F32, **kw)


def _ada_kernel(c_ref, w_ref, b_ref, o_ref):
    a = jax.nn.silu(c_ref[...]).astype(BF16)
    o_ref[...] = jnp.dot(a, w_ref[...].astype(BF16), preferred_element_type=F32) + b_ref[...]


def _ada_mods(c_all, ada_w, ada_b):
    depth, d, n = ada_w.shape
    rows = c_all.shape[0]
    tn = 512
    return pl.pallas_call(
        _ada_kernel, grid=(depth, n // tn),
        in_specs=[pl.BlockSpec((rows, d), lambda l, j: (0, 0)),
                  pl.BlockSpec((None, d, tn), lambda l, j: (l, 0, j)),
                  pl.BlockSpec((None, 1, tn), lambda l, j: (l, 0, j))],
        out_specs=pl.BlockSpec((None, rows, tn), lambda l, j: (l, 0, j)),
        out_shape=jax.ShapeDtypeStruct((depth, rows, n), F32),
        compiler_params=_params(2, 40), name="ada_mods",
    )(c_all, ada_w, ada_b.reshape(depth, 1, n))


class _Rows:
    def __init__(self, n_rows, tm, mods, rope_tab, seq):
        self.n_rows, self.tm, self.mods, self.rope_tab, self.seq = n_rows, tm, mods, rope_tab, seq
        self.per_row = mods.ndim == 4
        self.d = mods.shape[-1]

    def mod_spec(self, layer, k, tm=None):
        tm = tm or self.tm
        if self.per_row:
            return pl.BlockSpec((None, None, tm, self.d), lambda i: (layer, k, i, 0))
        seq = self.seq
        return pl.BlockSpec((None, None, None, 1, self.d), lambda i: (layer, (i * tm) // seq, k, 0, 0))

    def rope_spec(self, tm=None):
        tm = tm or self.tm
        w = self.rope_tab.shape[1]
        if self.per_row:
            return pl.BlockSpec((tm, w), lambda i: (i, 0))
        per_seq = self.seq // tm
        return pl.BlockSpec((tm, w), lambda i: (i % per_seq, 0))


def _modulate_kernel(x_ref, sc_ref, sh_ref, h_ref):
    h_ref[...] = (x_ref[...] * (1.0 + sc_ref[...]) + sh_ref[...]).astype(BF16)


def _modulate(rows, x, layer):
    tm, d = rows.tm, rows.d
    return pl.pallas_call(
        _modulate_kernel, grid=(rows.n_rows // tm,),
        in_specs=[pl.BlockSpec((tm, d), lambda i: (i, 0)), rows.mod_spec(layer, 1), rows.mod_spec(layer, 0)],
        out_specs=pl.BlockSpec((tm, d), lambda i: (i, 0)),
        out_shape=jax.ShapeDtypeStruct((rows.n_rows, d), BF16),
        compiler_params=_params(1, 32), name="modulate",
    )(x, rows.mods, rows.mods)


def _mm_kernel(*refs, ksizes):
    b_ref, o_ref = refs[len(ksizes)], refs[-1]
    acc, off = None, 0
    for a_ref, k in zip(refs, ksizes):
        part = jnp.dot(a_ref[...], b_ref[off:off + k, :], preferred_element_type=F32)
        acc = part if acc is None else acc + part
        off += k
    o_ref[...] = acc.astype(o_ref.dtype)


def _mm(a_list, b, layer, tm, tn, out_dtype, vmem_mb, name):
    m = a_list[0].shape[0]
    ks = tuple(a.shape[1] for a in a_list)
    k_all, n = b.shape[1], b.shape[2]
    assert sum(ks) == k_all and m % tm == 0 and n % tn == 0
    in_specs = [pl.BlockSpec((tm, k), lambda i, j: (i, 0)) for k in ks]
    in_specs.append(pl.BlockSpec((None, k_all, tn), lambda i, j: (layer, 0, j)))
    return pl.pallas_call(
        functools.partial(_mm_kernel, ksizes=ks), grid=(m // tm, n // tn), in_specs=in_specs,
        out_specs=pl.BlockSpec((tm, tn), lambda i, j: (i, j)),
        out_shape=jax.ShapeDtypeStruct((m, n), out_dtype),
        compiler_params=_params(2, vmem_mb), name=name,
    )(*a_list, b)


def _rope_tables(pos, dh_a, dk_b):
    lane = jnp.arange(LANES)

    def tabs(dh):
        r, half = dh // 4, dh // 8
        inv = ROPE_THETA ** (-jnp.arange(half, dtype=F32) * (2.0 / r))
        ang = pos.astype(F32)[:, None] * inv[None, :]
        cos, sin = jnp.cos(ang), jnp.sin(ang)
        ld = lane % dh
        idx = ld % half
        c = jnp.where(ld < r, cos[:, idx], 1.0)
        s_up = jnp.where((ld >= half) & (ld < r), sin[:, idx], 0.0)
        s_dn = jnp.where(ld < half, -sin[:, idx], 0.0)
        return [c, s_up, s_dn]

    return jnp.concatenate(tabs(dh_a) + tabs(dk_b), axis=1).astype(F32)


def _post_kernel(*refs, w_a, w_b, w_c, half_a, half_b, n_prev, stacked):
    p_ref, tab_ref, lcg_ref, lcb_ref = refs[:4]
    qa_ref, ka_ref, va_ref, qb_ref, kb_ref, vb_ref, uc_ref, vc_ref = refs[4 + n_prev:12 + n_prev]
    cache_refs = refs[12 + n_prev:] if stacked else (None,) * 4
    tab = tab_ref[...]
    t = [tab[:, i * LANES:(i + 1) * LANES] for i in range(6)]

    def rope(x, c, s_up, s_dn, half):
        return x * c + pltpu.roll(x, half, 1) * s_up + pltpu.roll(x, LANES - half, 1) * s_dn

    def split_cols(src_off, dst_ref, width, tabs=None, half=None, slab_ref=None, head_major=False):
        for j in range(width // LANES):
            x = p_ref[:, src_off + j * LANES: src_off + (j + 1) * LANES]
            if tabs is not None:
                x = rope(x, *tabs, half)
            dst_ref[:, j * LANES:(j + 1) * LANES] = x.astype(dst_ref.dtype)
            if slab_ref is not None and head_major:
                slab_ref[j] = x
            elif slab_ref is not None:
                slab_ref[:, j, :] = x

    off = 0
    split_cols(off, qa_ref, w_a, t[0:3], half_a); off += w_a
    split_cols(off, ka_ref, w_a, t[0:3], half_a, cache_refs[0], True); off += w_a
    split_cols(off, va_ref, w_a, slab_ref=cache_refs[1], head_major=True); off += w_a
    split_cols(off, qb_ref, w_b, t[3:6], half_b); off += w_b
    split_cols(off, kb_ref, w_b, t[3:6], half_b, cache_refs[2]); off += w_b
    split_cols(off, vb_ref, w_b, slab_ref=cache_refs[3]); off += w_b
    uc_ref[...] = jax.nn.gelu(p_ref[:, off:off + w_c]); off += w_c
    g = jax.nn.gelu(p_ref[:, off:off + w_c])
    mu = jnp.mean(g, axis=-1, keepdims=True)
    gc = g - mu
    var = jnp.mean(gc * gc, axis=-1, keepdims=True)
    vc_ref[...] = gc * lax.rsqrt(var + LN_EPS) * lcg_ref[...] + lcb_ref[...]


def _post(rows, proj, ln_c_g, ln_c_b, layer, w_a, w_b, w_c, dh_a, dk_b, stacked=False, prev=()):
    tm, n = rows.tm, rows.n_rows
    depth = ln_c_g.shape[0]
    widths = (w_a, w_a, w_a, w_b, w_b, w_b, w_c, w_c)
    kv = BF16 if stacked else F32
    dtypes = (BF16, kv, kv, BF16, kv, kv, F32, F32)
    row_spec = lambda w: pl.BlockSpec((tm, w), lambda i: (i, 0))
    in_specs = [row_spec(proj.shape[1]), rows.rope_spec(),
                pl.BlockSpec((None, 1, w_c), lambda i: (layer, 0, 0)),
                pl.BlockSpec((None, 1, w_c), lambda i: (layer, 0, 0))]
    out_specs = [row_spec(w) for w in widths]
    out_shape = [jax.ShapeDtypeStruct((n, w), dt) for w, dt in zip(widths, dtypes)]
    aliases = {}
    if stacked:
        seq, per_seq = rows.seq, rows.seq // tm
        for w in (w_a, w_a):
            heads = w // LANES
            out_specs.append(pl.BlockSpec((None, None, heads, tm, LANES),
                                          lambda i: (layer, i // per_seq, 0, i % per_seq, 0)))
            out_shape.append(jax.ShapeDtypeStruct((depth, n // seq, heads, seq, LANES), F32))
        for w in (w_b, w_b):
            heads = w // LANES
            out_specs.append(pl.BlockSpec((None, tm, heads, LANES), lambda i: (layer, i, 0, 0)))
            out_shape.append(jax.ShapeDtypeStruct((depth, n, heads, LANES), F32))
        in_specs += [pl.BlockSpec(memory_space=pl.ANY)] * len(prev)
        aliases = {4 + k: 8 + k for k in range(len(prev))}
    return pl.pallas_call(
        functools.partial(_post_kernel, w_a=w_a, w_b=w_b, w_c=w_c, half_a=dh_a // 8, half_b=dk_b // 8,
                          n_prev=len(prev), stacked=stacked),
        grid=(n // tm,), in_specs=in_specs, out_specs=out_specs, out_shape=out_shape,
        input_output_aliases=aliases,
        compiler_params=_params(1, 48), name="proj_split",
    )(proj, rows.rope_tab, ln_c_g.reshape(depth, 1, w_c), ln_c_b.reshape(depth, 1, w_c), *prev)


def _moba_kernel(q_ref, k_ref, v_ref, o_ref, km_ref, *, nb, scale):
    blk = MOBA_BLOCK
    qi = pl.program_id(2)

    @pl.when(qi == 0)
    def _():
        km_ref[...] = jnp.zeros_like(km_ref)
        for j in range(nb):
            km_ref[j:j + 1, :] = jnp.mean(k_ref[j * blk:(j + 1) * blk, :].astype(F32), axis=0, keepdims=True)

    q = q_ref[...]
    gate = _nt(q.astype(F32), km_ref[...], precision=HIGHEST)
    lane = lax.broadcasted_iota(jnp.int32, (blk, LANES), 1)
    g = jnp.where(lane < qi, gate, NEG_INF)
    rank = jnp.zeros((blk, LANES), F32)
    for m in range(nb):
        gm = g[:, m:m + 1]
        tie_first = jnp.where(lane > m, 1.0, 0.0)
        rank = rank + jnp.where(gm > g, 1.0, jnp.where(gm == g, tie_first, 0.0))
    sel = jnp.where(lane < qi, jnp.where(rank < MOBA_TOPK, 1.0, 0.0), 0.0)

    def block(j):
        start = pl.multiple_of(j * blk, blk)
        kj = k_ref[pl.ds(start, blk), :].astype(BF16)
        vj = v_ref[pl.ds(start, blk), :].astype(BF16)
        return _nt(q, kj) * scale, vj

    s, vj = block(qi)
    row = lax.broadcasted_iota(jnp.int32, (blk, blk), 0)
    col = lax.broadcasted_iota(jnp.int32, (blk, blk), 1)
    s = jnp.where(col <= row, s, NEG_INF)
    m0 = jnp.max(s, axis=1, keepdims=True)
    p = jnp.exp(s - m0)
    l0 = jnp.sum(p, axis=1, keepdims=True)
    acc0 = jnp.dot(p.astype(BF16), vj, preferred_element_type=F32)

    def body(j, carry):
        m_run, l_run, acc = carry
        s, vj = block(j)
        chosen = jnp.max(jnp.where(lane == j, sel, 0.0), axis=1, keepdims=True)
        s = jnp.where(chosen > 0.0, s, NEG_INF)
        m_new = jnp.maximum(m_run, jnp.max(s, axis=1, keepdims=True))
        alpha = jnp.exp(m_run - m_new)
        p = jnp.exp(s - m_new)
        l_new = alpha * l_run + jnp.sum(p, axis=1, keepdims=True)
        acc = alpha * acc + jnp.dot(p.astype(BF16), vj, preferred_element_type=F32)
        return m_new, l_new, acc

    _, l_fin, acc = lax.fori_loop(0, qi, body, (m0, l0, acc0))
    o_ref[...] = (acc / l_fin).astype(o_ref.dtype)


def _moba_prompt(q, k, v, batch, seq, heads):
    nb = seq // MOBA_BLOCK
    assert seq % MOBA_BLOCK == 0 and nb <= LANES
    return pl.pallas_call(
        functools.partial(_moba_kernel, nb=nb, scale=LANES ** -0.5),
        grid=(batch, heads, nb),
        in_specs=[pl.BlockSpec((MOBA_BLOCK, LANES), lambda b, h, i: (b * nb + i, h)),
                  pl.BlockSpec((seq, LANES), lambda b, h, i: (b, h)),
                  pl.BlockSpec((seq, LANES), lambda b, h, i: (b, h))],
        out_specs=pl.BlockSpec((MOBA_BLOCK, LANES), lambda b, h, i: (b * nb + i, h)),
        out_shape=jax.ShapeDtypeStruct(q.shape, BF16),
        scratch_shapes=[pltpu.VMEM((LANES, LANES), F32)],
        compiler_params=_params(3, 32), name="moba_prompt",
    )(q, k, v)


def _lam(lamp_ref, lam_init):
    lp = lamp_ref[...]
    d1 = jnp.sum(lp[0:1] * lp[1:2], axis=1, keepdims=True)
    d2 = jnp.sum(lp[2:3] * lp[3:4], axis=1, keepdims=True)
    return jnp.exp(d1) - jnp.exp(d2) + lam_init


def _diff_finish(o, gain, lam_init):
    ms = jnp.mean(o * o, axis=-1, keepdims=True)
    return o * lax.rsqrt(ms + LN_EPS) * gain * (1.0 - lam_init)


def _diff_kernel(lamp_ref, gain_ref, q_ref, k_ref, v_ref, o_ref, *, tq, dk, lam_init):
    qi = pl.program_id(2)
    scale = dk ** -0.5
    lam = _lam(lamp_ref, lam_init)
    q = q_ref[...]
    lane = lax.broadcasted_iota(jnp.int32, (tq, LANES), 1)
    zero = jnp.zeros_like(q)
    q1 = jnp.where(lane < dk, q, zero)
    q2 = jnp.where(lane >= dk, q, zero)

    def scores(j):
        start = pl.multiple_of(j * tq, tq)
        kj = k_ref[pl.ds(start, tq), :].astype(BF16)
        vj = v_ref[pl.ds(start, tq), :].astype(BF16)
        return _nt(q1, kj) * scale, _nt(q2, kj) * scale, vj

    def first(s, vj):
        m = jnp.max(s, axis=1, keepdims=True)
        p = jnp.exp(s - m)
        return m, jnp.sum(p, axis=1, keepdims=True), jnp.dot(p.astype(BF16), vj, preferred_element_type=F32)

    def update(state, s, vj):
        m_run, l_run, acc = state
        m_new = jnp.maximum(m_run, jnp.max(s, axis=1, keepdims=True))
        alpha = jnp.exp(m_run - m_new)
        p = jnp.exp(s - m_new)
        return (m_new, alpha * l_run + jnp.sum(p, axis=1, keepdims=True),
                alpha * acc + jnp.dot(p.astype(BF16), vj, preferred_element_type=F32))

    s1, s2, vj = scores(qi)
    row = lax.broadcasted_iota(jnp.int32, (tq, tq), 0)
    col = lax.broadcasted_iota(jnp.int32, (tq, tq), 1)
    causal = col <= row
    st1 = first(jnp.where(causal, s1, NEG_INF), vj)
    st2 = first(jnp.where(causal, s2, NEG_INF), vj)

    def body(j, carry):
        a, b = carry
        s1, s2, vj = scores(j)
        return update(a, s1, vj), update(b, s2, vj)

    st1, st2 = lax.fori_loop(0, qi, body, (st1, st2))
    o = st1[2] / st1[1] - lam * (st2[2] / st2[1])
    o_ref[...] = _diff_finish(o, gain_ref[...], lam_init).astype(o_ref.dtype)


def _diff_prompt(q, k, v, lamp, gain, layer, lam_init, batch, seq, heads, dk):
    tq = 256
    nq = seq // tq
    depth = lamp.shape[0]
    return pl.pallas_call(
        functools.partial(_diff_kernel, tq=tq, dk=dk, lam_init=lam_init),
        grid=(batch, heads, nq),
        in_specs=[pl.BlockSpec((None, 4, dk), lambda b, h, i: (layer, 0, 0)),
                  pl.BlockSpec((None, 1, LANES), lambda b, h, i: (layer, 0, 0)),
                  pl.BlockSpec((tq, LANES), lambda b, h, i: (b * nq + i, h)),
                  pl.BlockSpec((seq, LANES), lambda b, h, i: (b, h)),
                  pl.BlockSpec((seq, LANES), lambda b, h, i: (b, h))],
        out_specs=pl.BlockSpec((tq, LANES), lambda b, h, i: (b * nq + i, h)),
        out_shape=jax.ShapeDtypeStruct(q.shape, BF16),
        compiler_params=_params(3, 32), name="diff_prompt",
    )(lamp, gain.reshape(depth, 1, LANES), q, k, v)


def _gmlp_kernel(u_ref, v_ref, w_ref, bt_ref, o_ref, *, groups, chunk):
    row = lax.broadcasted_iota(jnp.int32, (chunk, chunk), 0)
    col = lax.broadcasted_iota(jnp.int32, (chunk, chunk), 1)
    causal = col <= row
    for g in range(groups):
        sl = slice(g * LANES, (g + 1) * LANES)
        w = jnp.where(causal, w_ref[g], 0.0).astype(BF16)
        mixed = jnp.dot(w, v_ref[:, sl].astype(BF16), preferred_element_type=F32) + bt_ref[:, g:g + 1]
        o_ref[:, sl] = (u_ref[:, sl] * mixed).astype(o_ref.dtype)


def _gmlp_prompt(u, v, w_s, b_st, layer):
    n, w_c = u.shape
    groups, chunk = w_s.shape[1], w_s.shape[2]
    return pl.pallas_call(
        functools.partial(_gmlp_kernel, groups=groups, chunk=chunk), grid=(n // chunk,),
        in_specs=[pl.BlockSpec((chunk, w_c), lambda i: (i, 0)), pl.BlockSpec((chunk, w_c), lambda i: (i, 0)),
                  pl.BlockSpec((None, groups, chunk, chunk), lambda i: (layer, 0, 0, 0)),
                  pl.BlockSpec((None, chunk, groups), lambda i: (layer, 0, 0))],
        out_specs=pl.BlockSpec((chunk, w_c), lambda i: (i, 0)),
        out_shape=jax.ShapeDtypeStruct((n, w_c), BF16),
        compiler_params=_params(1, 32), name="gmlp_prompt",
    )(u, v, w_s, b_st)


def _gmlp_first_kernel(u_ref, v_ref, w_ref, bt_ref, o_ref, *, groups):
    for g in range(groups):
        sl = slice(g * LANES, (g + 1) * LANES)
        mixed = w_ref[g, 0:1, 0:1] * v_ref[:, sl] + bt_ref[0:1, g:g + 1]
        o_ref[:, sl] = (u_ref[:, sl] * mixed).astype(o_ref.dtype)


def _gmlp_sample(u, v, w_s, b_st, layer):
    n, w_c = u.shape
    groups, chunk = w_s.shape[1], w_s.shape[2]
    return pl.pallas_call(
        functools.partial(_gmlp_first_kernel, groups=groups), grid=(1,),
        in_specs=[pl.BlockSpec((n, w_c), lambda i: (0, 0)), pl.BlockSpec((n, w_c), lambda i: (0, 0)),
                  pl.BlockSpec((None, groups, chunk, chunk), lambda i: (layer, 0, 0, 0)),
                  pl.BlockSpec((None, chunk, groups), lambda i: (layer, 0, 0))],
        out_specs=pl.BlockSpec((n, w_c), lambda i: (0, 0)),
        out_shape=jax.ShapeDtypeStruct((n, w_c), BF16),
        compiler_params=_params(1, 32), name="gmlp_sample",
    )(u, v, w_s, b_st)


def _ln_kernel(*refs, alpha, emit_h):
    x_ref, y_ref, gate_ref, g_ref, b_ref = refs[:5]
    z = alpha * x_ref[...] + gate_ref[...] * y_ref[...]
    mu = jnp.mean(z, axis=-1, keepdims=True)
    zc = z - mu
    var = jnp.mean(zc * zc, axis=-1, keepdims=True)
    xn = zc * lax.rsqrt(var + LN_EPS) * g_ref[...] + b_ref[...]
    if emit_h:
        sc_ref, sh_ref, xo_ref, h_ref = refs[5:]
        h_ref[...] = (xn * (1.0 + sc_ref[...]) + sh_ref[...]).astype(BF16)
    else:
        xo_ref = refs[5]
    xo_ref[...] = xn


def _ln(rows, x, y, ln_g, ln_b, layer, gate_k, alpha, next_mod=None):
    tm, d, n = min(rows.tm, 256), rows.d, rows.n_rows
    depth = ln_g.shape[0]
    row_spec = pl.BlockSpec((tm, d), lambda i: (i, 0))
    par_spec = pl.BlockSpec((None, 1, d), lambda i: (layer, 0, 0))
    in_specs = [row_spec, row_spec, rows.mod_spec(layer, gate_k, tm), par_spec, par_spec]
    args = [x, y, rows.mods, ln_g.reshape(depth, 1, d), ln_b.reshape(depth, 1, d)]
    out_specs, out_shape = [row_spec], [jax.ShapeDtypeStruct((n, d), F32)]
    if next_mod is not None:
        nl, k_sc, k_sh = next_mod
        in_specs += [rows.mod_spec(nl, k_sc, tm), rows.mod_spec(nl, k_sh, tm)]
        args += [rows.mods, rows.mods]
        out_specs.append(row_spec)
        out_shape.append(jax.ShapeDtypeStruct((n, d), BF16))
    out = pl.pallas_call(
        functools.partial(_ln_kernel, alpha=alpha, emit_h=next_mod is not None), grid=(n // tm,),
        in_specs=in_specs, out_specs=out_specs, out_shape=out_shape,
        compiler_params=_params(1, 48), name="deepnorm_ln",
    )(*args)
    return (out[0], out[1]) if next_mod is not None else (out[0], None)


def _peer_select_kernel(q_ref, keys_ref, s1_ref, e1_ref, s2_ref, e2_ref, tau_ref, cand_ref, *, heads, n_keys, pairs):
    tm = q_ref.shape[0]
    sub = lax.broadcasted_iota(jnp.int32, (n_keys, tm), 0)
    csub = lax.broadcasted_iota(jnp.int32, cand_ref.shape, 0)

    def pop_max(x, iota, big):
        m = jnp.max(x, axis=0, keepdims=True)
        first = jnp.min(jnp.where(x == m, iota, big), axis=0, keepdims=True)
        return m, jnp.where(iota == first, NEG_INF, x)

    def top_rows(s):
        rows, x = [], s
        for _ in range(PEER_TOPK):
            m, x = pop_max(x, sub, n_keys)
            rows.append(m)
        return rows

    for h in range(heads):
        s1 = _nt(keys_ref[0], q_ref[:, (2 * h) * n_keys:(2 * h + 1) * n_keys], precision=HIGHEST)
        s2 = _nt(keys_ref[1], q_ref[:, (2 * h + 1) * n_keys:(2 * h + 2) * n_keys], precision=HIGHEST)
        a1, a2 = top_rows(s1), top_rows(s2)
        cand_ref[...] = jnp.full(cand_ref.shape, NEG_INF, F32)
        for r, (i, j) in enumerate(pairs):
            cand_ref[r:r + 1, :] = a1[i] + a2[j]
        c = cand_ref[...]
        x, tau = c, None
        for _ in range(PEER_TOPK):
            tau, x = pop_max(x, csub, cand_ref.shape[0])
        cmax = a1[0] + a2[0]
        z = jnp.sum(jnp.where(c >= tau, jnp.exp(c - cmax), 0.0), axis=0, keepdims=True)
        s1_ref[h] = s1
        s2_ref[h] = s2
        e1_ref[h] = jnp.exp(s1 - a1[0]) / z
        e2_ref[h] = jnp.exp(s2 - a2[0])
        tau_ref[h] = tau


def _peer_select(q, keys, layer, heads, tm):
    n, n_keys = q.shape[0], keys.shape[2]
    pairs = tuple((i, j) for i in range(PEER_TOPK) for j in range(PEER_TOPK) if (i + 1) * (j + 1) <= PEER_TOPK)
    n_cand = -(-len(pairs) // 8) * 8
    big = pl.BlockSpec((heads, n_keys, tm), lambda i: (0, 0, i))
    big_shape = jax.ShapeDtypeStruct((heads, n_keys, n), F32)
    return pl.pallas_call(
        functools.partial(_peer_select_kernel, heads=heads, n_keys=n_keys, pairs=pairs), grid=(n // tm,),
        in_specs=[pl.BlockSpec((tm, q.shape[1]), lambda i: (i, 0)),
                  pl.BlockSpec((None, 2, n_keys, keys.shape[3]), lambda i: (layer, 0, 0, 0))],
        out_specs=[big, big, big, big, pl.BlockSpec((heads, 1, tm), lambda i: (0, 0, i))],
        out_shape=[big_shape] * 4 + [jax.ShapeDtypeStruct((heads, 1, n), F32)],
        scratch_shapes=[pltpu.VMEM((n_cand, tm), F32)],
        compiler_params=_params(1, 32), name="peer_select",
    )(q, keys)


def _peer_dense_kernel(ht_ref, u_ref, vt_ref, s1_ref, e1_ref, s2_ref, e2_ref, tau_ref, o_ref, w_ref, g_ref,
                       *, heads, n_keys, n1_per_step):
    j = pl.program_id(1)
    tm = ht_ref.shape[1]

    @pl.when(j == 0)
    def _():
        o_ref[...] = jnp.zeros_like(o_ref)

    for r0 in range(0, n1_per_step, 2):
        pair = slice(r0 * n_keys, (r0 + 2) * n_keys)
        s1_rows = [[s1_ref[h, pl.ds(j * n1_per_step + r0 + k, 1), :] for k in range(2)] for h in range(heads)]
        e1_rows = [[e1_ref[h, pl.ds(j * n1_per_step + r0 + k, 1), :] for k in range(2)] for h in range(heads)]
        for c in range(tm // LANES):
            cols = slice(c * LANES, (c + 1) * LANES)
            g = [None, None]
            for h in range(heads):
                s2, e2, tau = s2_ref[h, :, cols], e2_ref[h, :, cols], tau_ref[h, :, cols]
                for k in range(2):
                    term = e1_rows[h][k][:, cols] * jnp.where(s1_rows[h][k][:, cols] + s2 >= tau, e2, 0.0)
                    g[k] = term if g[k] is None else g[k] + term
            for k in range(2):
                g_ref[(r0 + k) * n_keys:(r0 + k + 1) * n_keys, cols] = g[k]
        st = jnp.dot(u_ref[pair, :], ht_ref[...], preferred_element_type=F32)
        w_ref[pair, :] = (g_ref[pair, :] * jax.nn.gelu(st)).astype(BF16)
    rows_per_chunk = 512
    for m in range(vt_ref.shape[0] // rows_per_chunk):
        rows = slice(m * rows_per_chunk, (m + 1) * rows_per_chunk)
        o_ref[rows, :] += jnp.dot(vt_ref[rows, :], w_ref[...], preferred_element_type=F32)


def _peer_dense(h2t, u, vt, sel, layer, heads, tm, n1_per_step, vmem_mb):
    d, n = h2t.shape
    n_keys = sel[0].shape[1]
    et = n1_per_step * n_keys
    n_steps = n_keys * n_keys // et
    assert n1_per_step % 2 == 0 and n_keys % n1_per_step == 0
    once = pl.Buffered(1)
    resident = pl.BlockSpec((heads, n_keys, tm), lambda i, j: (0, 0, i), pipeline_mode=once)
    return pl.pallas_call(
        functools.partial(_peer_dense_kernel, heads=heads, n_keys=n_keys, n1_per_step=n1_per_step),
        grid=(n // tm, n_steps),
        in_specs=[pl.BlockSpec((d, tm), lambda i, j: (0, i), pipeline_mode=once),
                  pl.BlockSpec((None, et, d), lambda i, j: (layer, j, 0)),
                  pl.BlockSpec((None, d, et), lambda i, j: (layer, 0, j)),
                  resident, resident, resident, resident,
                  pl.BlockSpec((heads, 1, tm), lambda i, j: (0, 0, i))],
        out_specs=pl.BlockSpec((d, tm), lambda i, j: (0, i)),
        out_shape=jax.ShapeDtypeStruct((d, n), F32),
        scratch_shapes=[pltpu.VMEM((et, tm), BF16), pltpu.VMEM((et, tm), F32)],
        compiler_params=_params(2, vmem_mb), name="peer_dense",
    )(h2t, u, vt, *sel)


def _moba_decode_kernel(pt_ref, q_ref, kn_ref, vn_ref, k_ref, v_ref, o_ref, ks_ref, m_ref, l_ref, acc_ref,
                        *, pages_per_block, n_pages, scale):
    del pt_ref
    p = pl.program_id(1)
    blk = p // pages_per_block
    heads, page = k_ref.shape[0], k_ref.shape[1]
    qf = q_ref[...].astype(F32)

    @pl.when(p % pages_per_block == 0)
    def _():
        m_ref[blk] = jnp.full(qf.shape, NEG_INF, F32)
        l_ref[blk] = jnp.zeros_like(qf)
        acc_ref[blk] = jnp.zeros_like(qf)
        ks_ref[blk] = jnp.zeros_like(qf)

    for h in range(heads):
        row = slice(h, h + 1)
        kh, vh = k_ref[h], v_ref[h]
        q16 = jnp.broadcast_to(qf[row, :], (16, LANES)).astype(BF16)
        s = _nt(q16, kh.astype(BF16))[0:1, :] * scale
        m_old = m_ref[blk, row, :]
        m_new = jnp.maximum(m_old, jnp.max(s, axis=1, keepdims=True))
        alpha = jnp.exp(m_old - m_new)
        pe = jnp.exp(s - m_new)
        pv = jnp.dot(jnp.broadcast_to(pe.astype(BF16), (16, page)), vh.astype(BF16), preferred_element_type=F32)
        m_ref[blk, row, :] = m_new
        l_ref[blk, row, :] = alpha * l_ref[blk, row, :] + jnp.sum(pe, axis=1, keepdims=True)
        acc_ref[blk, row, :] = alpha * acc_ref[blk, row, :] + pv[0:1, :]
        ks_ref[blk, row, :] = ks_ref[blk, row, :] + jnp.sum(kh, axis=0, keepdims=True)

    @pl.when(p == n_pages - 1)
    def _():
        shape = ks_ref.shape
        nb = shape[0]
        gate = jnp.sum(ks_ref[...] * (1.0 / MOBA_BLOCK) * qf, axis=-1, keepdims=True)
        g = jnp.broadcast_to(gate, shape)
        bi = lax.broadcasted_iota(jnp.int32, shape, 0)
        chosen = jnp.zeros(shape, F32)
        for _ in range(MOBA_TOPK):
            gmax = jnp.max(g, axis=0)
            first = jnp.min(jnp.where(g == gmax, bi, nb), axis=0)
            hit = bi == first
            chosen = jnp.where(hit, 1.0, chosen)
            g = jnp.where(hit, NEG_INF, g)
        s_own = jnp.broadcast_to(jnp.sum(qf * kn_ref[...], axis=-1, keepdims=True) * scale, qf.shape)
        m_all = m_ref[...]
        m_fin = jnp.maximum(s_own, jnp.max(jnp.where(chosen > 0.0, m_all, NEG_INF), axis=0))
        wts = jnp.where(chosen > 0.0, jnp.exp(m_all - m_fin), 0.0)
        w_own = jnp.exp(s_own - m_fin)
        l_fin = jnp.sum(wts * l_ref[...], axis=0) + w_own
        o = jnp.sum(wts * acc_ref[...], axis=0) + w_own * vn_ref[...]
        o_ref[...] = (o / l_fin).astype(o_ref.dtype)


def _moba_decode(page_table, q3, kn3, vn3, cache_k, cache_v, layer):
    batch, n_pages = page_table.shape
    heads, page = cache_k.shape[2], cache_k.shape[3]
    ppb = MOBA_BLOCK // page
    nb = n_pages // ppb
    assert n_pages % ppb == 0 and nb >= MOBA_TOPK and page == LANES
    row = pl.BlockSpec((None, heads, LANES), lambda b, p, pt: (b, 0, 0))
    page_spec = pl.BlockSpec((None, None, heads, page, LANES), lambda b, p, pt: (layer, pt[b, p], 0, 0, 0))
    return pl.pallas_call(
        functools.partial(_moba_decode_kernel, pages_per_block=ppb, n_pages=n_pages, scale=LANES ** -0.5),
        grid_spec=pltpu.PrefetchScalarGridSpec(
            num_scalar_prefetch=1, grid=(batch, n_pages),
            in_specs=[row, row, row, page_spec, page_spec],
            out_specs=row,
            scratch_shapes=[pltpu.VMEM((nb, heads, LANES), F32)] * 4),
        out_shape=jax.ShapeDtypeStruct((batch, heads, LANES), BF16),
        compiler_params=_params(2, 32), name="moba_decode",
    )(page_table, q3, kn3, vn3, cache_k, cache_v)


def _diff_decode_kernel(pt_ref, lamp_ref, gain_ref, q_ref, kn_ref, vn_ref, k_ref, v_ref, o_ref, m_ref, l_ref, acc_ref,
                        *, dk, n_pages, lam_init):
    del pt_ref
    p = pl.program_id(1)
    scale = dk ** -0.5
    qf = q_ref[...].astype(F32)
    heads = qf.shape[0]
    spread = jnp.where(lax.broadcasted_iota(jnp.int32, (LANES, 2 * LANES), 0) // dk
                       == lax.broadcasted_iota(jnp.int32, (LANES, 2 * LANES), 1) // LANES, 1.0, 0.0).astype(BF16)

    def map_scores(prod):
        hi = prod.astype(BF16)
        lo = (prod - hi.astype(F32)).astype(BF16)
        s = (jnp.dot(hi, spread, preferred_element_type=F32) + jnp.dot(lo, spread, preferred_element_type=F32)) * scale
        return s[:, :LANES], s[:, LANES:]

    @pl.when(p == 0)
    def _():
        s_new = map_scores(kn_ref[...] * qf)
        for i in range(2):
            m_ref[i] = s_new[i]
            l_ref[i] = jnp.ones_like(qf)
            acc_ref[i] = vn_ref[...]

    k3, v3 = k_ref[...], v_ref[...]
    page = k3.shape[0]
    scores = map_scores((k3 * qf).reshape(page * heads, LANES))
    for i in range(2):
        s = scores[i].reshape(page, heads, LANES)
        m_old = m_ref[i]
        m_new = jnp.maximum(m_old, jnp.max(s, axis=0))
        alpha = jnp.exp(m_old - m_new)
        pe = jnp.exp(s - m_new)
        m_ref[i] = m_new
        l_ref[i] = alpha * l_ref[i] + jnp.sum(pe, axis=0)
        acc_ref[i] = alpha * acc_ref[i] + jnp.sum(pe * v3, axis=0)

    @pl.when(p == n_pages - 1)
    def _():
        lam = _lam(lamp_ref, lam_init)
        o = acc_ref[0] / l_ref[0] - lam * (acc_ref[1] / l_ref[1])
        o_ref[...] = _diff_finish(o, gain_ref[...], lam_init).astype(o_ref.dtype)


def _diff_decode(page_table, q3, kn3, vn3, cache_k, cache_v, lamp, gain, layer, lam_init, dk):
    batch, n_pages = page_table.shape
    page, heads = cache_k.shape[2], cache_k.shape[3]
    depth = lamp.shape[0]
    row = pl.BlockSpec((None, heads, LANES), lambda b, p, pt: (b, 0, 0))
    page_spec = pl.BlockSpec((None, None, page, heads, LANES), lambda b, p, pt: (layer, pt[b, p], 0, 0, 0))
    return pl.pallas_call(
        functools.partial(_diff_decode_kernel, dk=dk, n_pages=n_pages, lam_init=lam_init),
        grid_spec=pltpu.PrefetchScalarGridSpec(
            num_scalar_prefetch=1, grid=(batch, n_pages),
            in_specs=[pl.BlockSpec((None, 4, dk), lambda b, p, pt: (layer, 0, 0)),
                      pl.BlockSpec((None, 1, LANES), lambda b, p, pt: (layer, 0, 0)),
                      row, row, row, page_spec, page_spec],
            out_specs=row,
            scratch_shapes=[pltpu.VMEM((2, heads, LANES), F32)] * 3),
        out_shape=jax.ShapeDtypeStruct((batch, heads, LANES), BF16),
        compiler_params=_params(2, 32), name="diff_decode",
    )(page_table, lamp, gain.reshape(depth, 1, LANES), q3, kn3, vn3, cache_k, cache_v)


def kernel(x_prompt, x_sample, cache_a_k, cache_a_v, cache_b_k, cache_b_v, page_table, c_prompt, c_sample, ada_w, ada_b, w_in, w_o, lam_q1, lam_k1, lam_q2, lam_k2, diff_gain, ln_c_g, ln_c_b, w_s, b_s, ln1_g, ln1_b, ln2_g, ln2_b, peer_wq, peer_subkeys, peer_u, peer_v):
    batch, seq, d = x_prompt.shape
    dec_batch, dec_seq, _ = x_sample.shape
    depth = w_in.shape[0]
    n_phys, page, h_a, dh_a = cache_a_k.shape[1:]
    h_b, dv_b = cache_b_v.shape[3:]
    dk_b = cache_b_k.shape[4] // 2
    w_a, w_b, w_c = h_a * dh_a, h_b * dv_b, ln_c_g.shape[1]
    n_keys = peer_subkeys.shape[2]
    peer_heads = peer_wq.shape[2] // (2 * peer_subkeys.shape[3])
    past_len = page_table.shape[1] * page
    assert dec_seq == 1 and dh_a == LANES and dv_b == LANES and n_keys == LANES and peer_subkeys.shape[3] == LANES
    assert past_len % MOBA_BLOCK == 0 and w_s.shape[2] == LANES and dec_batch <= SAMPLE_ROWS
    alpha = (2 * depth) ** 0.25
    n_p, n_s = batch * seq, SAMPLE_ROWS

    w_in_b, w_o_b, wq_b = w_in.astype(BF16), w_o.astype(BF16), peer_wq.astype(BF16)
    u_b, vt_b = peer_u.astype(BF16), jnp.swapaxes(peer_v, 1, 2).astype(BF16)
    lamp = jnp.stack([lam_q1, lam_k1, lam_q2, lam_k2], axis=1)
    b_st = jnp.swapaxes(b_s, 1, 2)
    ck_a, cv_a = jnp.swapaxes(cache_a_k, 2, 3), jnp.swapaxes(cache_a_v, 2, 3)

    c_rows = -(-(batch + dec_batch) // 16) * 16
    c_all = jnp.zeros((c_rows, d), F32).at[:batch].set(c_prompt).at[batch:batch + dec_batch].set(c_sample)
    mods = _ada_mods(c_all, ada_w, ada_b)
    mods_p = mods[:, :batch].reshape(depth, batch, 6, 1, d)
    mods_s = mods[:, batch:batch + dec_batch].reshape(depth, dec_batch, 6, d).transpose(0, 2, 1, 3)
    mods_s = jnp.pad(mods_s, ((0, 0), (0, 0), (0, n_s - dec_batch), (0, 0)))

    rows_p = _Rows(n_p, 128, mods_p, _rope_tables(jnp.arange(seq), dh_a, dk_b), seq)
    rows_s = _Rows(n_s, n_s, mods_s, _rope_tables(jnp.full((n_s,), past_len), dh_a, dk_b), 1)

    x_p = x_prompt.reshape(n_p, d)
    x_s = jnp.pad(x_sample.reshape(dec_batch, d), ((0, n_s - dec_batch), (0, 0)))
    h_p = _modulate(rows_p, x_p, 0)
    h_s = _modulate(rows_s, x_s, 0)

    caches_p, outs_s = (), []
    for l in range(depth):
        lam_init = 0.8 - 0.6 * math.exp(-0.3 * l)
        nxt = (l + 1, 1, 0) if l + 1 < depth else None

        def mixer_inputs(rows, h, **cache_args):
            tm = _tile(rows.n_rows, 1024)
            proj = _mm([h], w_in_b, l, tm, _tile(w_in.shape[2], 768), F32, 48, "in_proj")
            return _post(rows, proj, ln_c_g, ln_c_b, l, w_a, w_b, w_c, dh_a, dk_b, **cache_args)

        def channel_mix(rows, x, mix_parts):
            tm = _tile(rows.n_rows, 1024)
            mix = _mm(mix_parts, w_o_b, l, tm, _tile(d, 512), F32, 48, "out_proj")
            x1, h2 = _ln(rows, x, mix, ln1_g, ln1_b, l, 2, alpha, (l, 4, 3))
            pq = _mm([h2], wq_b, l, tm, 512, F32, 48, "peer_query")
            sel = _peer_select(pq, peer_subkeys, l, peer_heads, _tile(rows.n_rows, 256))
            ffn_t = _peer_dense(h2.T, u_b, vt_b, sel, l, peer_heads, _tile(rows.n_rows, 512), 4, 56)
            return _ln(rows, x1, ffn_t.T, ln2_g, ln2_b, l, 5, alpha, nxt)

        qa, ka, va, qb, kb, vb, uc, vc, *caches_p = mixer_inputs(rows_p, h_p, stacked=True, prev=tuple(caches_p))
        oa = _moba_prompt(qa, ka, va, batch, seq, h_a)
        ob = _diff_prompt(qb, kb, vb, lamp, diff_gain, l, lam_init, batch, seq, h_b, dk_b)
        oc = _gmlp_prompt(uc, vc, w_s, b_st, l)
        x_p, h_p = channel_mix(rows_p, x_p, [oa, ob, oc])

        qa, ka, va, qb, kb, vb, uc, vc = mixer_inputs(rows_s, h_s)
        r3 = lambda a, heads: a[:dec_batch].reshape(dec_batch, heads, LANES)
        oa = _moba_decode(page_table, r3(qa, h_a), r3(ka, h_a), r3(va, h_a), ck_a, cv_a, l)
        ob = _diff_decode(page_table, r3(qb, h_b), r3(kb, h_b), r3(vb, h_b), cache_b_k, cache_b_v, lamp, diff_gain,
                          l, lam_init, dk_b)
        oc = _gmlp_sample(uc, vc, w_s, b_st, l)
        pad_rows = lambda a: jnp.pad(a.reshape(dec_batch, -1), ((0, n_s - dec_batch), (0, 0)))
        x_s, h_s = channel_mix(rows_s, x_s, [pad_rows(oa), pad_rows(ob), oc])
        outs_s.append((ka[:dec_batch], va[:dec_batch], kb[:dec_batch], vb[:dec_batch], vc[:dec_batch]))

    stack = lambda rows, i: jnp.stack([r[i] for r in rows], axis=0)
    return (x_p.reshape(batch, seq, d),
            x_s[:dec_batch].reshape(dec_batch, 1, d),
            jnp.swapaxes(caches_p[0], 2, 3),
            jnp.swapaxes(caches_p[1], 2, 3),
            caches_p[2].reshape(depth, batch, seq, h_b, 2 * dk_b),
            caches_p[3].reshape(depth, batch, seq, h_b, dv_b),
            stack(outs_s, 0).reshape(depth, dec_batch, 1, h_a, dh_a),
            stack(outs_s, 1).reshape(depth, dec_batch, 1, h_a, dh_a),
            stack(outs_s, 2).reshape(depth, dec_batch, 1, h_b, 2 * dk_b),
            stack(outs_s, 3).reshape(depth, dec_batch, 1, h_b, dv_b),
            stack(outs_s, 4).reshape(depth, dec_batch, 1, w_c))
```

```python
import functools
import math

import jax
import jax.numpy as jnp
from jax import lax
from jax.experimental import pallas as pl
from jax.experimental.pallas import tpu as pltpu

F32 = jnp.float32
BF16 = jnp.bfloat16
HIGHEST = lax.Precision.HIGHEST
NEG_INF = float("-inf")

LANES = 128
MOBA_BLOCK = 256
MOBA_TOPK = 3
PEER_TOPK = 16
ROPE_THETA = 500000.0
LN_EPS = 1e-5
SAMPLE_ROWS = 128
VMEM_MB = 1024 * 1024


def _params(n_axes, vmem_mb):
    return pltpu.CompilerParams(dimension_semantics=("arbitrary",) * n_axes,
                                vmem_limit_bytes=vmem_mb * VMEM_MB)


def _tile(n, preferred):
    t = min(n, preferred) // LANES * LANES
    while n % t:
        t -= LANES
    return t


def _nt(a, b, **kw):
    return lax.dot_general(a, b, (((1,), (1,)), ((), ())), preferred_element_type=F32, **kw)


def _ada_kernel(c_ref, w_ref, b_ref, o_ref):
    a = jax.nn.silu(c_ref[...]).astype(BF16)
    o_ref[...] = jnp.dot(a, w_ref[...].astype(BF16), preferred_element_type=F32) + b_ref[...]


def _ada_mods(c_all, ada_w, ada_b):
    depth, d, n = ada_w.shape
    rows = c_all.shape[0]
    tn = 512
    return pl.pallas_call(
        _ada_kernel, grid=(depth, n // tn),
        in_specs=[pl.BlockSpec((rows, d), lambda l, j: (0, 0)),
                  pl.BlockSpec((None, d, tn), lambda l, j: (l, 0, j)),
                  pl.BlockSpec((None, 1, tn), lambda l, j: (l, 0, j))],
        out_specs=pl.BlockSpec((None, rows, tn), lambda l, j: (l, 0, j)),
        out_shape=jax.ShapeDtypeStruct((depth, rows, n), F32),
        compiler_params=_params(2, 40), name="ada_mods",
    )(c_all, ada_w, ada_b.reshape(depth, 1, n))


class _Rows:
    def __init__(self, n_rows, tm, mods, rope_tab, seq):
        self.n_rows, self.tm, self.mods, self.rope_tab, self.seq = n_rows, tm, mods, rope_tab, seq
        self.per_row = mods.ndim == 4
        self.d = mods.shape[-1]

    def mod_spec(self, layer, k, tm=None):
        tm = tm or self.tm
        if self.per_row:
            return pl.BlockSpec((None, None, tm, self.d), lambda i: (layer, k, i, 0))
        seq = self.seq
        return pl.BlockSpec((None, None, None, 1, self.d), lambda i: (layer, (i * tm) // seq, k, 0, 0))

    def rope_spec(self, tm=None):
        tm = tm or self.tm
        w = self.rope_tab.shape[1]
        if self.per_row:
            return pl.BlockSpec((tm, w), lambda i: (i, 0))
        per_seq = self.seq // tm
        return pl.BlockSpec((tm, w), lambda i: (i % per_seq, 0))


def _modulate_kernel(x_ref, sc_ref, sh_ref, h_ref):
    h_ref[...] = (x_ref[...] * (1.0 + sc_ref[...]) + sh_ref[...]).astype(BF16)


def _modulate(rows, x, layer):
    tm, d = rows.tm, rows.d
    return pl.pallas_call(
        _modulate_kernel, grid=(rows.n_rows // tm,),
        in_specs=[pl.BlockSpec((tm, d), lambda i: (i, 0)), rows.mod_spec(layer, 1), rows.mod_spec(layer, 0)],
        out_specs=pl.BlockSpec((tm, d), lambda i: (i, 0)),
        out_shape=jax.ShapeDtypeStruct((rows.n_rows, d), BF16),
        compiler_params=_params(1, 32), name="modulate",
    )(x, rows.mods, rows.mods)


def _mm_kernel(*refs, ksizes):
    b_ref, o_ref = refs[len(ksizes)], refs[-1]
    acc, off = None, 0
    for a_ref, k in zip(refs, ksizes):
        part = jnp.dot(a_ref[...], b_ref[off:off + k, :], preferred_element_type=F32)
        acc = part if acc is None else acc + part
        off += k
    o_ref[...] = acc.astype(o_ref.dtype)


def _mm(a_list, b, layer, tm, tn, out_dtype, vmem_mb, name):
    m = a_list[0].shape[0]
    ks = tuple(a.shape[1] for a in a_list)
    k_all, n = b.shape[1], b.shape[2]
    assert sum(ks) == k_all and m % tm == 0 and n % tn == 0
    in_specs = [pl.BlockSpec((tm, k), lambda i, j: (i, 0)) for k in ks]
    in_specs.append(pl.BlockSpec((None, k_all, tn), lambda i, j: (layer, 0, j)))
    return pl.pallas_call(
        functools.partial(_mm_kernel, ksizes=ks), grid=(m // tm, n // tn), in_specs=in_specs,
        out_specs=pl.BlockSpec((tm, tn), lambda i, j: (i, j)),
        out_shape=jax.ShapeDtypeStruct((m, n), out_dtype),
        compiler_params=_params(2, vmem_mb), name=name,
    )(*a_list, b)


def _rope_tables(pos, dh_a, dk_b):
    lane = jnp.arange(LANES)

    def tabs(dh):
        r, half = dh // 4, dh // 8
        inv = ROPE_THETA ** (-jnp.arange(half, dtype=F32) * (2.0 / r))
        ang = pos.astype(F32)[:, None] * inv[None, :]
        cos, sin = jnp.cos(ang), jnp.sin(ang)
        ld = lane % dh
        idx = ld % half
        c = jnp.where(ld < r, cos[:, idx], 1.0)
        s_up = jnp.where((ld >= half) & (ld < r), sin[:, idx], 0.0)
        s_dn = jnp.where(ld < half, -sin[:, idx], 0.0)
        return [c, s_up, s_dn]

    return jnp.concatenate(tabs(dh_a) + tabs(dk_b), axis=1).astype(F32)


def _post_kernel(*refs, w_a, w_b, w_c, half_a, half_b, n_prev, stacked):
    p_ref, tab_ref, lcg_ref, lcb_ref = refs[:4]
    qa_ref, ka_ref, va_ref, qb_ref, kb_ref, vb_ref, uc_ref, vc_ref = refs[4 + n_prev:12 + n_prev]
    cache_refs = refs[12 + n_prev:] if stacked else (None,) * 4
    tab = tab_ref[...]
    t = [tab[:, i * LANES:(i + 1) * LANES] for i in range(6)]

    def rope(x, c, s_up, s_dn, half):
        return x * c + pltpu.roll(x, half, 1) * s_up + pltpu.roll(x, LANES - half, 1) * s_dn

    def split_cols(src_off, dst_ref, width, tabs=None, half=None, slab_ref=None, head_major=False):
        for j in range(width // LANES):
            x = p_ref[:, src_off + j * LANES: src_off + (j + 1) * LANES]
            if tabs is not None:
                x = rope(x, *tabs, half)
            dst_ref[:, j * LANES:(j + 1) * LANES] = x.astype(dst_ref.dtype)
            if slab_ref is not None and head_major:
                slab_ref[j] = x
            elif slab_ref is not None:
                slab_ref[:, j, :] = x

    off = 0
    split_cols(off, qa_ref, w_a, t[0:3], half_a); off += w_a
    split_cols(off, ka_ref, w_a, t[0:3], half_a, cache_refs[0], True); off += w_a
    split_cols(off, va_ref, w_a, slab_ref=cache_refs[1], head_major=True); off += w_a
    split_cols(off, qb_ref, w_b, t[3:6], half_b); off += w_b
    split_cols(off, kb_ref, w_b, t[3:6], half_b, cache_refs[2]); off += w_b
    split_cols(off, vb_ref, w_b, slab_ref=cache_refs[3]); off += w_b
    uc_ref[...] = jax.nn.gelu(p_ref[:, off:off + w_c]); off += w_c
    g = jax.nn.gelu(p_ref[:, off:off + w_c])
    mu = jnp.mean(g, axis=-1, keepdims=True)
    gc = g - mu
    var = jnp.mean(gc * gc, axis=-1, keepdims=True)
    vc_ref[...] = gc * lax.rsqrt(var + LN_EPS) * lcg_ref[...] + lcb_ref[...]


def _post(rows, proj, ln_c_g, ln_c_b, layer, w_a, w_b, w_c, dh_a, dk_b, stacked=False, prev=()):
    tm, n = rows.tm, rows.n_rows
    depth = ln_c_g.shape[0]
    widths = (w_a, w_a, w_a, w_b, w_b, w_b, w_c, w_c)
    kv = BF16 if stacked else F32
    dtypes = (BF16, kv, kv, BF16, kv, kv, F32, F32)
    row_spec = lambda w: pl.BlockSpec((tm, w), lambda i: (i, 0))
    in_specs = [row_spec(proj.shape[1]), rows.rope_spec(),
                pl.BlockSpec((None, 1, w_c), lambda i: (layer, 0, 0)),
                pl.BlockSpec((None, 1, w_c), lambda i: (layer, 0, 0))]
    out_specs = [row_spec(w) for w in widths]
    out_shape = [jax.ShapeDtypeStruct((n, w), dt) for w, dt in zip(widths, dtypes)]
    aliases = {}
    if stacked:
        seq, per_seq = rows.seq, rows.seq // tm
        for w in (w_a, w_a):
            heads = w // LANES
            out_specs.append(pl.BlockSpec((None, None, heads, tm, LANES),
                                          lambda i: (layer, i // per_seq, 0, i % per_seq, 0)))
            out_shape.append(jax.ShapeDtypeStruct((depth, n // seq, heads, seq, LANES), F32))
        for w in (w_b, w_b):
            heads = w // LANES
            out_specs.append(pl.BlockSpec((None, tm, heads, LANES), lambda i: (layer, i, 0, 0)))
            out_shape.append(jax.ShapeDtypeStruct((depth, n, heads, LANES), F32))
        in_specs += [pl.BlockSpec(memory_space=pl.ANY)] * len(prev)
        aliases = {4 + k: 8 + k for k in range(len(prev))}
    return pl.pallas_call(
        functools.partial(_post_kernel, w_a=w_a, w_b=w_b, w_c=w_c, half_a=dh_a // 8, half_b=dk_b // 8,
                          n_prev=len(prev), stacked=stacked),
        grid=(n // tm,), in_specs=in_specs, out_specs=out_specs, out_shape=out_shape,
        input_output_aliases=aliases,
        compiler_params=_params(1, 48), name="proj_split",
    )(proj, rows.rope_tab, ln_c_g.reshape(depth, 1, w_c), ln_c_b.reshape(depth, 1, w_c), *prev)


def _heads_per_step(heads, most=4):
    return max(h for h in range(1, most + 1) if heads % h == 0)


def _moba_kernel(q_ref, k_ref, v_ref, o_ref, km_ref, *, nb, scale, hp):
    blk = MOBA_BLOCK
    qi = pl.program_id(2)

    @pl.when(qi == 0)
    def _():
        km_ref[...] = jnp.zeros_like(km_ref)
        for h in range(hp):
            for j in range(nb):
                kb = k_ref[j * blk:(j + 1) * blk, h * LANES:(h + 1) * LANES].astype(F32)
                km_ref[h, j:j + 1, :] = jnp.mean(kb, axis=0, keepdims=True)

    lane = lax.broadcasted_iota(jnp.int32, (blk, LANES), 1)
    row = lax.broadcasted_iota(jnp.int32, (blk, blk), 0)
    col = lax.broadcasted_iota(jnp.int32, (blk, blk), 1)

    def block(h, q, j):
        start = pl.multiple_of(j * blk, blk)
        kj = k_ref[pl.ds(start, blk), h * LANES:(h + 1) * LANES].astype(BF16)
        vj = v_ref[pl.ds(start, blk), h * LANES:(h + 1) * LANES].astype(BF16)
        return _nt(q, kj) * scale, vj

    qs, sels, states = [], [], []
    for h in range(hp):
        q = q_ref[:, h * LANES:(h + 1) * LANES]
        gate = _nt(q.astype(F32), km_ref[h], precision=HIGHEST)
        g = jnp.where(lane < qi, gate, NEG_INF)
        rank = jnp.zeros((blk, LANES), F32)
        for m in range(nb):
            gm = g[:, m:m + 1]
            tie_first = jnp.where(lane > m, 1.0, 0.0)
            rank = rank + jnp.where(gm > g, 1.0, jnp.where(gm == g, tie_first, 0.0))
        sels.append(jnp.where(lane < qi, jnp.where(rank < MOBA_TOPK, 1.0, 0.0), 0.0))
        s, vj = block(h, q, qi)
        s = jnp.where(col <= row, s, NEG_INF)
        m0 = jnp.max(s, axis=1, keepdims=True)
        p = jnp.exp(s - m0)
        states.append((m0, jnp.sum(p, axis=1, keepdims=True), jnp.dot(p.astype(BF16), vj, preferred_element_type=F32)))
        qs.append(q)

    def body(j, carry):
        out = []
        for h, (m_run, l_run, acc) in enumerate(carry):
            s, vj = block(h, qs[h], j)
            chosen = jnp.max(jnp.where(lane == j, sels[h], 0.0), axis=1, keepdims=True)
            s = jnp.where(chosen > 0.0, s, NEG_INF)
            m_new = jnp.maximum(m_run, jnp.max(s, axis=1, keepdims=True))
            alpha = jnp.exp(m_run - m_new)
            p = jnp.exp(s - m_new)
            out.append((m_new, alpha * l_run + jnp.sum(p, axis=1, keepdims=True),
                        alpha * acc + jnp.dot(p.astype(BF16), vj, preferred_element_type=F32)))
        return tuple(out)

    final = lax.fori_loop(0, qi, body, tuple(states))
    for h, (_, l_fin, acc) in enumerate(final):
        o_ref[:, h * LANES:(h + 1) * LANES] = (acc / l_fin).astype(o_ref.dtype)


def _moba_prompt(q, k, v, batch, seq, heads):
    nb = seq // MOBA_BLOCK
    hp = _heads_per_step(heads)
    assert seq % MOBA_BLOCK == 0 and nb <= LANES
    return pl.pallas_call(
        functools.partial(_moba_kernel, nb=nb, scale=LANES ** -0.5, hp=hp),
        grid=(batch, heads // hp, nb),
        in_specs=[pl.BlockSpec((MOBA_BLOCK, hp * LANES), lambda b, g, i: (b * nb + i, g)),
                  pl.BlockSpec((seq, hp * LANES), lambda b, g, i: (b, g)),
                  pl.BlockSpec((seq, hp * LANES), lambda b, g, i: (b, g))],
        out_specs=pl.BlockSpec((MOBA_BLOCK, hp * LANES), lambda b, g, i: (b * nb + i, g)),
        out_shape=jax.ShapeDtypeStruct(q.shape, BF16),
        scratch_shapes=[pltpu.VMEM((hp, LANES, LANES), F32)],
        compiler_params=_params(3, 32), name="moba_prompt",
    )(q, k, v)


def _lam(lamp_ref, lam_init):
    lp = lamp_ref[...]
    d1 = jnp.sum(lp[0:1] * lp[1:2], axis=1, keepdims=True)
    d2 = jnp.sum(lp[2:3] * lp[3:4], axis=1, keepdims=True)
    return jnp.exp(d1) - jnp.exp(d2) + lam_init


def _diff_finish(o, gain, lam_init):
    ms = jnp.mean(o * o, axis=-1, keepdims=True)
    return o * lax.rsqrt(ms + LN_EPS) * gain * (1.0 - lam_init)


def _diff_kernel(lamp_ref, gain_ref, q_ref, k_ref, v_ref, o_ref, *, tq, dk, lam_init, hp):
    qi = pl.program_id(2)
    scale = dk ** -0.5
    lam = _lam(lamp_ref, lam_init)
    lane = lax.broadcasted_iota(jnp.int32, (tq, LANES), 1)
    causal = lax.broadcasted_iota(jnp.int32, (tq, tq), 1) <= lax.broadcasted_iota(jnp.int32, (tq, tq), 0)

    def scores(h, qq, j):
        start = pl.multiple_of(j * tq, tq)
        kj = k_ref[pl.ds(start, tq), h * LANES:(h + 1) * LANES].astype(BF16)
        vj = v_ref[pl.ds(start, tq), h * LANES:(h + 1) * LANES].astype(BF16)
        return _nt(qq[0], kj) * scale, _nt(qq[1], kj) * scale, vj

    def first(s, vj):
        m = jnp.max(s, axis=1, keepdims=True)
        p = jnp.exp(s - m)
        return m, jnp.sum(p, axis=1, keepdims=True), jnp.dot(p.astype(BF16), vj, preferred_element_type=F32)

    def update(state, s, vj):
        m_run, l_run, acc = state
        m_new = jnp.maximum(m_run, jnp.max(s, axis=1, keepdims=True))
        alpha = jnp.exp(m_run - m_new)
        p = jnp.exp(s - m_new)
        return (m_new, alpha * l_run + jnp.sum(p, axis=1, keepdims=True),
                alpha * acc + jnp.dot(p.astype(BF16), vj, preferred_element_type=F32))

    qqs, states = [], []
    for h in range(hp):
        q = q_ref[:, h * LANES:(h + 1) * LANES]
        zero = jnp.zeros_like(q)
        qq = (jnp.where(lane < dk, q, zero), jnp.where(lane >= dk, q, zero))
        s1, s2, vj = scores(h, qq, qi)
        states.append((first(jnp.where(causal, s1, NEG_INF), vj), first(jnp.where(causal, s2, NEG_INF), vj)))
        qqs.append(qq)

    def body(j, carry):
        out = []
        for h, (a, b) in enumerate(carry):
            s1, s2, vj = scores(h, qqs[h], j)
            out.append((update(a, s1, vj), update(b, s2, vj)))
        return tuple(out)

    final = lax.fori_loop(0, qi, body, tuple(states))
    for h, (st1, st2) in enumerate(final):
        o = st1[2] / st1[1] - lam * (st2[2] / st2[1])
        o_ref[:, h * LANES:(h + 1) * LANES] = _diff_finish(o, gain_ref[...], lam_init).astype(o_ref.dtype)


def _diff_prompt(q, k, v, lamp, gain, layer, lam_init, batch, seq, heads, dk):
    tq = 256
    nq = seq // tq
    depth = lamp.shape[0]
    hp = _heads_per_step(heads)
    return pl.pallas_call(
        functools.partial(_diff_kernel, tq=tq, dk=dk, lam_init=lam_init, hp=hp),
        grid=(batch, heads // hp, nq),
        in_specs=[pl.BlockSpec((None, 4, dk), lambda b, g, i: (layer, 0, 0)),
                  pl.BlockSpec((None, 1, LANES), lambda b, g, i: (layer, 0, 0)),
                  pl.BlockSpec((tq, hp * LANES), lambda b, g, i: (b * nq + i, g)),
                  pl.BlockSpec((seq, hp * LANES), lambda b, g, i: (b, g)),
                  pl.BlockSpec((seq, hp * LANES), lambda b, g, i: (b, g))],
        out_specs=pl.BlockSpec((tq, hp * LANES), lambda b, g, i: (b * nq + i, g)),
        out_shape=jax.ShapeDtypeStruct(q.shape, BF16),
        compiler_params=_params(3, 32), name="diff_prompt",
    )(lamp, gain.reshape(depth, 1, LANES), q, k, v)


def _gmlp_kernel(u_ref, v_ref, w_ref, bt_ref, o_ref, *, groups, chunk):
    row = lax.broadcasted_iota(jnp.int32, (chunk, chunk), 0)
    col = lax.broadcasted_iota(jnp.int32, (chunk, chunk), 1)
    causal = col <= row
    for g in range(groups):
        sl = slice(g * LANES, (g + 1) * LANES)
        w = jnp.where(causal, w_ref[g], 0.0).astype(BF16)
        mixed = jnp.dot(w, v_ref[:, sl].astype(BF16), preferred_element_type=F32) + bt_ref[:, g:g + 1]
        o_ref[:, sl] = (u_ref[:, sl] * mixed).astype(o_ref.dtype)


def _gmlp_prompt(u, v, w_s, b_st, layer):
    n, w_c = u.shape
    groups, chunk = w_s.shape[1], w_s.shape[2]
    return pl.pallas_call(
        functools.partial(_gmlp_kernel, groups=groups, chunk=chunk), grid=(n // chunk,),
        in_specs=[pl.BlockSpec((chunk, w_c), lambda i: (i, 0)), pl.BlockSpec((chunk, w_c), lambda i: (i, 0)),
                  pl.BlockSpec((None, groups, chunk, chunk), lambda i: (layer, 0, 0, 0)),
                  pl.BlockSpec((None, chunk, groups), lambda i: (layer, 0, 0))],
        out_specs=pl.BlockSpec((chunk, w_c), lambda i: (i, 0)),
        out_shape=jax.ShapeDtypeStruct((n, w_c), BF16),
        compiler_params=_params(1, 32), name="gmlp_prompt",
    )(u, v, w_s, b_st)


def _gmlp_first_kernel(u_ref, v_ref, w_ref, bt_ref, o_ref, *, groups):
    for g in range(groups):
        sl = slice(g * LANES, (g + 1) * LANES)
        mixed = w_ref[g, 0:1, 0:1] * v_ref[:, sl] + bt_ref[0:1, g:g + 1]
        o_ref[:, sl] = (u_ref[:, sl] * mixed).astype(o_ref.dtype)


def _gmlp_sample(u, v, w_s, b_st, layer):
    n, w_c = u.shape
    groups, chunk = w_s.shape[1], w_s.shape[2]
    return pl.pallas_call(
        functools.partial(_gmlp_first_kernel, groups=groups), grid=(1,),
        in_specs=[pl.BlockSpec((n, w_c), lambda i: (0, 0)), pl.BlockSpec((n, w_c), lambda i: (0, 0)),
                  pl.BlockSpec((None, groups, chunk, chunk), lambda i: (layer, 0, 0, 0)),
                  pl.BlockSpec((None, chunk, groups), lambda i: (layer, 0, 0))],
        out_specs=pl.BlockSpec((n, w_c), lambda i: (0, 0)),
        out_shape=jax.ShapeDtypeStruct((n, w_c), BF16),
        compiler_params=_params(1, 32), name="gmlp_sample",
    )(u, v, w_s, b_st)


def _ln_kernel(*refs, alpha, emit_h):
    x_ref, y_ref, gate_ref, g_ref, b_ref = refs[:5]
    z = alpha * x_ref[...] + gate_ref[...] * y_ref[...]
    mu = jnp.mean(z, axis=-1, keepdims=True)
    zc = z - mu
    var = jnp.mean(zc * zc, axis=-1, keepdims=True)
    xn = zc * lax.rsqrt(var + LN_EPS) * g_ref[...] + b_ref[...]
    if emit_h:
        sc_ref, sh_ref, xo_ref, h_ref = refs[5:]
        h_ref[...] = (xn * (1.0 + sc_ref[...]) + sh_ref[...]).astype(BF16)
    else:
        xo_ref = refs[5]
    xo_ref[...] = xn


def _ln(rows, x, y, ln_g, ln_b, layer, gate_k, alpha, next_mod=None):
    tm, d, n = min(rows.tm, 256), rows.d, rows.n_rows
    depth = ln_g.shape[0]
    row_spec = pl.BlockSpec((tm, d), lambda i: (i, 0))
    par_spec = pl.BlockSpec((None, 1, d), lambda i: (layer, 0, 0))
    in_specs = [row_spec, row_spec, rows.mod_spec(layer, gate_k, tm), par_spec, par_spec]
    args = [x, y, rows.mods, ln_g.reshape(depth, 1, d), ln_b.reshape(depth, 1, d)]
    out_specs, out_shape = [row_spec], [jax.ShapeDtypeStruct((n, d), F32)]
    if next_mod is not None:
        nl, k_sc, k_sh = next_mod
        in_specs += [rows.mod_spec(nl, k_sc, tm), rows.mod_spec(nl, k_sh, tm)]
        args += [rows.mods, rows.mods]
        out_specs.append(row_spec)
        out_shape.append(jax.ShapeDtypeStruct((n, d), BF16))
    out = pl.pallas_call(
        functools.partial(_ln_kernel, alpha=alpha, emit_h=next_mod is not None), grid=(n // tm,),
        in_specs=in_specs, out_specs=out_specs, out_shape=out_shape,
        compiler_params=_params(1, 48), name="deepnorm_ln",
    )(*args)
    return (out[0], out[1]) if next_mod is not None else (out[0], None)


def _peer_select_kernel(q_ref, keys_ref, s1_ref, e1_ref, s2_ref, e2_ref, tau_ref, cand_ref, *, heads, n_keys, pairs):
    tm = q_ref.shape[0]
    sub = lax.broadcasted_iota(jnp.int32, (n_keys, tm), 0)
    csub = lax.broadcasted_iota(jnp.int32, cand_ref.shape, 0)

    def pop_max(x, iota, big):
        m = jnp.max(x, axis=0, keepdims=True)
        first = jnp.min(jnp.where(x == m, iota, big), axis=0, keepdims=True)
        return m, jnp.where(iota == first, NEG_INF, x)

    def top_rows(s):
        rows, x = [], s
        for _ in range(PEER_TOPK):
            m, x = pop_max(x, sub, n_keys)
            rows.append(m)
        return rows

    for h in range(heads):
        s1 = _nt(keys_ref[0], q_ref[:, (2 * h) * n_keys:(2 * h + 1) * n_keys], precision=HIGHEST)
        s2 = _nt(keys_ref[1], q_ref[:, (2 * h + 1) * n_keys:(2 * h + 2) * n_keys], precision=HIGHEST)
        a1, a2 = top_rows(s1), top_rows(s2)
        cand_ref[...] = jnp.full(cand_ref.shape, NEG_INF, F32)
        for r, (i, j) in enumerate(pairs):
            cand_ref[r:r + 1, :] = a1[i] + a2[j]
        c = cand_ref[...]
        x, tau = c, None
        for _ in range(PEER_TOPK):
            tau, x = pop_max(x, csub, cand_ref.shape[0])
        cmax = a1[0] + a2[0]
        z = jnp.sum(jnp.where(c >= tau, jnp.exp(c - cmax), 0.0), axis=0, keepdims=True)
        s1_ref[h] = s1
        s2_ref[h] = s2
        e1_ref[h] = jnp.exp(s1 - a1[0]) / z
        e2_ref[h] = jnp.exp(s2 - a2[0])
        tau_ref[h] = tau


def _peer_select(q, keys, layer, heads, tm):
    n, n_keys = q.shape[0], keys.shape[2]
    pairs = tuple((i, j) for i in range(PEER_TOPK) for j in range(PEER_TOPK) if (i + 1) * (j + 1) <= PEER_TOPK)
    n_cand = -(-len(pairs) // 8) * 8
    big = pl.BlockSpec((heads, n_keys, tm), lambda i: (0, 0, i))
    big_shape = jax.ShapeDtypeStruct((heads, n_keys, n), F32)
    return pl.pallas_call(
        functools.partial(_peer_select_kernel, heads=heads, n_keys=n_keys, pairs=pairs), grid=(n // tm,),
        in_specs=[pl.BlockSpec((tm, q.shape[1]), lambda i: (i, 0)),
                  pl.BlockSpec((None, 2, n_keys, keys.shape[3]), lambda i: (layer, 0, 0, 0))],
        out_specs=[big, big, big, big, pl.BlockSpec((heads, 1, tm), lambda i: (0, 0, i))],
        out_shape=[big_shape] * 4 + [jax.ShapeDtypeStruct((heads, 1, n), F32)],
        scratch_shapes=[pltpu.VMEM((n_cand, tm), F32)],
        compiler_params=_params(1, 32), name="peer_select",
    )(q, keys)


def _peer_dense_kernel(ht_ref, u_ref, vt_ref, s1_ref, e1_ref, s2_ref, e2_ref, tau_ref, o_ref, w_ref, g_ref,
                       *, heads, n_keys, n1_per_step):
    j = pl.program_id(1)
    tm = ht_ref.shape[1]

    @pl.when(j == 0)
    def _():
        o_ref[...] = jnp.zeros_like(o_ref)

    for r0 in range(0, n1_per_step, 2):
        pair = slice(r0 * n_keys, (r0 + 2) * n_keys)
        s1_rows = [[s1_ref[h, pl.ds(j * n1_per_step + r0 + k, 1), :] for k in range(2)] for h in range(heads)]
        e1_rows = [[e1_ref[h, pl.ds(j * n1_per_step + r0 + k, 1), :] for k in range(2)] for h in range(heads)]
        for c in range(tm // LANES):
            cols = slice(c * LANES, (c + 1) * LANES)
            g = [None, None]
            for h in range(heads):
                s2, e2, tau = s2_ref[h, :, cols], e2_ref[h, :, cols], tau_ref[h, :, cols]
                for k in range(2):
                    term = e1_rows[h][k][:, cols] * jnp.where(s1_rows[h][k][:, cols] + s2 >= tau, e2, 0.0)
                    g[k] = term if g[k] is None else g[k] + term
            for k in range(2):
                g_ref[(r0 + k) * n_keys:(r0 + k + 1) * n_keys, cols] = g[k]
        st = jnp.dot(u_ref[pair, :], ht_ref[...], preferred_element_type=F32)
        w_ref[pair, :] = (g_ref[pair, :] * jax.nn.gelu(st)).astype(BF16)
    rows_per_chunk = 512
    for m in range(vt_ref.shape[0] // rows_per_chunk):
        rows = slice(m * rows_per_chunk, (m + 1) * rows_per_chunk)
        o_ref[rows, :] += jnp.dot(vt_ref[rows, :], w_ref[...], preferred_element_type=F32)


def _peer_dense(h2t, u, vt, sel, layer, heads, tm, n1_per_step, vmem_mb):
    d, n = h2t.shape
    n_keys = sel[0].shape[1]
    et = n1_per_step * n_keys
    n_steps = n_keys * n_keys // et
    assert n1_per_step % 2 == 0 and n_keys % n1_per_step == 0
    once = pl.Buffered(1)
    resident = pl.BlockSpec((heads, n_keys, tm), lambda i, j: (0, 0, i), pipeline_mode=once)
    return pl.pallas_call(
        functools.partial(_peer_dense_kernel, heads=heads, n_keys=n_keys, n1_per_step=n1_per_step),
        grid=(n // tm, n_steps),
        in_specs=[pl.BlockSpec((d, tm), lambda i, j: (0, i), pipeline_mode=once),
                  pl.BlockSpec((None, et, d), lambda i, j: (layer, j, 0)),
                  pl.BlockSpec((None, d, et), lambda i, j: (layer, 0, j)),
                  resident, resident, resident, resident,
                  pl.BlockSpec((heads, 1, tm), lambda i, j: (0, 0, i))],
        out_specs=pl.BlockSpec((d, tm), lambda i, j: (0, i)),
        out_shape=jax.ShapeDtypeStruct((d, n), F32),
        scratch_shapes=[pltpu.VMEM((et, tm), BF16), pltpu.VMEM((et, tm), F32)],
        compiler_params=_params(2, vmem_mb), name="peer_dense",
    )(h2t, u, vt, *sel)


def _moba_decode_kernel(pt_ref, q_ref, kn_ref, vn_ref, k_ref, v_ref, o_ref, ks_ref, m_ref, l_ref, acc_ref,
                        *, pages_per_block, n_pages, scale):
    del pt_ref
    p = pl.program_id(1)
    blk = p // pages_per_block
    heads, page = k_ref.shape[0], k_ref.shape[1]
    qf = q_ref[...].astype(F32)

    @pl.when(p % pages_per_block == 0)
    def _():
        m_ref[blk] = jnp.full(qf.shape, NEG_INF, F32)
        l_ref[blk] = jnp.zeros_like(qf)
        acc_ref[blk] = jnp.zeros_like(qf)
        ks_ref[blk] = jnp.zeros_like(qf)

    s_rows, ksum_rows = [], []
    for h in range(heads):
        kh = k_ref[h]
        q16 = jnp.broadcast_to(qf[h:h + 1, :], (16, LANES)).astype(BF16)
        s_rows.append(_nt(q16, kh.astype(BF16))[0:1, :])
        ksum_rows.append(jnp.sum(kh, axis=0, keepdims=True))
    s = jnp.concatenate(s_rows, axis=0) * scale
    m_old = m_ref[blk]
    m_new = jnp.maximum(m_old, jnp.max(s, axis=1, keepdims=True))
    alpha = jnp.exp(m_old - m_new)
    pe = jnp.exp(s - m_new)
    pv_rows = [jnp.dot(jnp.broadcast_to(pe[h:h + 1, :], (16, page)).astype(BF16), v_ref[h].astype(BF16),
                       preferred_element_type=F32)[0:1, :] for h in range(heads)]
    m_ref[blk] = m_new
    l_ref[blk] = alpha * l_ref[blk] + jnp.sum(pe, axis=1, keepdims=True)
    acc_ref[blk] = alpha * acc_ref[blk] + jnp.concatenate(pv_rows, axis=0)
    ks_ref[blk] = ks_ref[blk] + jnp.concatenate(ksum_rows, axis=0)

    @pl.when(p == n_pages - 1)
    def _():
        shape = ks_ref.shape
        nb = shape[0]
        gate = jnp.sum(ks_ref[...] * (1.0 / MOBA_BLOCK) * qf, axis=-1, keepdims=True)
        g = jnp.broadcast_to(gate, shape)
        bi = lax.broadcasted_iota(jnp.int32, shape, 0)
        chosen = jnp.zeros(shape, F32)
        for _ in range(MOBA_TOPK):
            gmax = jnp.max(g, axis=0)
            first = jnp.min(jnp.where(g == gmax, bi, nb), axis=0)
            hit = bi == first
            chosen = jnp.where(hit, 1.0, chosen)
            g = jnp.where(hit, NEG_INF, g)
        s_own = jnp.broadcast_to(jnp.sum(qf * kn_ref[...], axis=-1, keepdims=True) * scale, qf.shape)
        m_all = m_ref[...]
        m_fin = jnp.maximum(s_own, jnp.max(jnp.where(chosen > 0.0, m_all, NEG_INF), axis=0))
        wts = jnp.where(chosen > 0.0, jnp.exp(m_all - m_fin), 0.0)
        w_own = jnp.exp(s_own - m_fin)
        l_fin = jnp.sum(wts * l_ref[...], axis=0) + w_own
        o = jnp.sum(wts * acc_ref[...], axis=0) + w_own * vn_ref[...]
        o_ref[...] = (o / l_fin).astype(o_ref.dtype)


def _moba_decode(page_table, q3, kn3, vn3, cache_k, cache_v, layer):
    batch, n_pages = page_table.shape
    heads, page = cache_k.shape[2], cache_k.shape[3]
    ppb = MOBA_BLOCK // page
    nb = n_pages // ppb
    assert n_pages % ppb == 0 and nb >= MOBA_TOPK and page == LANES
    row = pl.BlockSpec((None, heads, LANES), lambda b, p, pt: (b, 0, 0))
    page_spec = pl.BlockSpec((None, None, heads, page, LANES), lambda b, p, pt: (layer, pt[b, p], 0, 0, 0))
    return pl.pallas_call(
        functools.partial(_moba_decode_kernel, pages_per_block=ppb, n_pages=n_pages, scale=LANES ** -0.5),
        grid_spec=pltpu.PrefetchScalarGridSpec(
            num_scalar_prefetch=1, grid=(batch, n_pages),
            in_specs=[row, row, row, page_spec, page_spec],
            out_specs=row,
            scratch_shapes=[pltpu.VMEM((nb, heads, LANES), F32)] * 4),
        out_shape=jax.ShapeDtypeStruct((batch, heads, LANES), BF16),
        compiler_params=_params(2, 32), name="moba_decode",
    )(page_table, q3, kn3, vn3, cache_k, cache_v)


def _diff_decode_kernel(pt_ref, lamp_ref, gain_ref, q_ref, kn_ref, vn_ref, k_ref, v_ref, o_ref, m_ref, l_ref, acc_ref,
                        *, dk, n_pages, lam_init):
    del pt_ref
    p = pl.program_id(1)
    scale = dk ** -0.5
    qf = q_ref[...].astype(F32)
    heads = qf.shape[0]
    spread = jnp.where(lax.broadcasted_iota(jnp.int32, (LANES, 2 * LANES), 0) // dk
                       == lax.broadcasted_iota(jnp.int32, (LANES, 2 * LANES), 1) // LANES, 1.0, 0.0).astype(BF16)

    def map_scores(prod):
        hi = prod.astype(BF16)
        lo = (prod - hi.astype(F32)).astype(BF16)
        s = (jnp.dot(hi, spread, preferred_element_type=F32) + jnp.dot(lo, spread, preferred_element_type=F32)) * scale
        return s[:, :LANES], s[:, LANES:]

    @pl.when(p == 0)
    def _():
        s_new = map_scores(kn_ref[...] * qf)
        for i in range(2):
            m_ref[i] = s_new[i]
            l_ref[i] = jnp.ones_like(qf)
            acc_ref[i] = vn_ref[...]

    k3, v3 = k_ref[...], v_ref[...]
    page = k3.shape[0]
    scores = map_scores((k3 * qf).reshape(page * heads, LANES))
    for i in range(2):
        s = scores[i].reshape(page, heads, LANES)
        m_old = m_ref[i]
        m_new = jnp.maximum(m_old, jnp.max(s, axis=0))
        alpha = jnp.exp(m_old - m_new)
        pe = jnp.exp(s - m_new)
        m_ref[i] = m_new
        l_ref[i] = alpha * l_ref[i] + jnp.sum(pe, axis=0)
        acc_ref[i] = alpha * acc_ref[i] + jnp.sum(pe * v3, axis=0)

    @pl.when(p == n_pages - 1)
    def _():
        lam = _lam(lamp_ref, lam_init)
        o = acc_ref[0] / l_ref[0] - lam * (acc_ref[1] / l_ref[1])
        o_ref[...] = _diff_finish(o, gain_ref[...], lam_init).astype(o_ref.dtype)


def _diff_decode(page_table, q3, kn3, vn3, cache_k, cache_v, lamp, gain, layer, lam_init, dk):
    batch, n_pages = page_table.shape
    page, heads = cache_k.shape[2], cache_k.shape[3]
    depth = lamp.shape[0]
    row = pl.BlockSpec((None, heads, LANES), lambda b, p, pt: (b, 0, 0))
    page_spec = pl.BlockSpec((None, None, page, heads, LANES), lambda b, p, pt: (layer, pt[b, p], 0, 0, 0))
    return pl.pallas_call(
        functools.partial(_diff_decode_kernel, dk=dk, n_pages=n_pages, lam_init=lam_init),
        grid_spec=pltpu.PrefetchScalarGridSpec(
            num_scalar_prefetch=1, grid=(batch, n_pages),
            in_specs=[pl.BlockSpec((None, 4, dk), lambda b, p, pt: (layer, 0, 0)),
                      pl.BlockSpec((None, 1, LANES), lambda b, p, pt: (layer, 0, 0)),
                      row, row, row, page_spec, page_spec],
            out_specs=row,
            scratch_shapes=[pltpu.VMEM((2, heads, LANES), F32)] * 3),
        out_shape=jax.ShapeDtypeStruct((batch, heads, LANES), BF16),
        compiler_params=_params(2, 32), name="diff_decode",
    )(page_table, lamp, gain.reshape(depth, 1, LANES), q3, kn3, vn3, cache_k, cache_v)


def kernel(x_prompt, x_sample, cache_a_k, cache_a_v, cache_b_k, cache_b_v, page_table, c_prompt, c_sample, ada_w, ada_b, w_in, w_o, lam_q1, lam_k1, lam_q2, lam_k2, diff_gain, ln_c_g, ln_c_b, w_s, b_s, ln1_g, ln1_b, ln2_g, ln2_b, peer_wq, peer_subkeys, peer_u, peer_v):
    batch, seq, d = x_prompt.shape
    dec_batch, dec_seq, _ = x_sample.shape
    depth = w_in.shape[0]
    n_phys, page, h_a, dh_a = cache_a_k.shape[1:]
    h_b, dv_b = cache_b_v.shape[3:]
    dk_b = cache_b_k.shape[4] // 2
    w_a, w_b, w_c = h_a * dh_a, h_b * dv_b, ln_c_g.shape[1]
    n_keys = peer_subkeys.shape[2]
    peer_heads = peer_wq.shape[2] // (2 * peer_subkeys.shape[3])
    past_len = page_table.shape[1] * page
    assert dec_seq == 1 and dh_a == LANES and dv_b == LANES and n_keys == LANES and peer_subkeys.shape[3] == LANES
    assert past_len % MOBA_BLOCK == 0 and w_s.shape[2] == LANES and dec_batch <= SAMPLE_ROWS
    alpha = (2 * depth) ** 0.25
    n_p, n_s = batch * seq, SAMPLE_ROWS

    w_in_b, w_o_b, wq_b = w_in.astype(BF16), w_o.astype(BF16), peer_wq.astype(BF16)
    u_b, vt_b = peer_u.astype(BF16), jnp.swapaxes(peer_v, 1, 2).astype(BF16)
    lamp = jnp.stack([lam_q1, lam_k1, lam_q2, lam_k2], axis=1)
    b_st = jnp.swapaxes(b_s, 1, 2)
    ck_a, cv_a = jnp.swapaxes(cache_a_k, 2, 3), jnp.swapaxes(cache_a_v, 2, 3)

    c_rows = -(-(batch + dec_batch) // 16) * 16
    c_all = jnp.zeros((c_rows, d), F32).at[:batch].set(c_prompt).at[batch:batch + dec_batch].set(c_sample)
    mods = _ada_mods(c_all, ada_w, ada_b)
    mods_p = mods[:, :batch].reshape(depth, batch, 6, 1, d)
    mods_s = mods[:, batch:batch + dec_batch].reshape(depth, dec_batch, 6, d).transpose(0, 2, 1, 3)
    mods_s = jnp.pad(mods_s, ((0, 0), (0, 0), (0, n_s - dec_batch), (0, 0)))

    rows_p = _Rows(n_p, 128, mods_p, _rope_tables(jnp.arange(seq), dh_a, dk_b), seq)
    rows_s = _Rows(n_s, n_s, mods_s, _rope_tables(jnp.full((n_s,), past_len), dh_a, dk_b), 1)

    x_p = x_prompt.reshape(n_p, d)
    x_s = jnp.pad(x_sample.reshape(dec_batch, d), ((0, n_s - dec_batch), (0, 0)))
    h_p = _modulate(rows_p, x_p, 0)
    h_s = _modulate(rows_s, x_s, 0)

    caches_p, outs_s = (), []
    for l in range(depth):
        lam_init = 0.8 - 0.6 * math.exp(-0.3 * l)
        nxt = (l + 1, 1, 0) if l + 1 < depth else None

        def mixer_inputs(rows, h, **cache_args):
            tm = _tile(rows.n_rows, 1024)
            proj = _mm([h], w_in_b, l, tm, _tile(w_in.shape[2], 768), F32, 48, "in_proj")
            return _post(rows, proj, ln_c_g, ln_c_b, l, w_a, w_b, w_c, dh_a, dk_b, **cache_args)

        def channel_mix(rows, x, mix_parts):
            tm = _tile(rows.n_rows, 1024)
            mix = _mm(mix_parts, w_o_b, l, tm, _tile(d, 512), F32, 48, "out_proj")
            x1, h2 = _ln(rows, x, mix, ln1_g, ln1_b, l, 2, alpha, (l, 4, 3))
            pq = _mm([h2], wq_b, l, tm, 512, F32, 48, "peer_query")
            sel = _peer_select(pq, peer_subkeys, l, peer_heads, _tile(rows.n_rows, 256))
            ffn_t = _peer_dense(h2.T, u_b, vt_b, sel, l, peer_heads, _tile(rows.n_rows, 512), 4, 56)
            return _ln(rows, x1, ffn_t.T, ln2_g, ln2_b, l, 5, alpha, nxt)

        qa, ka, va, qb, kb, vb, uc, vc, *caches_p = mixer_inputs(rows_p, h_p, stacked=True, prev=tuple(caches_p))
        oa = _moba_prompt(qa, ka, va, batch, seq, h_a)
        ob = _diff_prompt(qb, kb, vb, lamp, diff_gain, l, lam_init, batch, seq, h_b, dk_b)
        oc = _gmlp_prompt(uc, vc, w_s, b_st, l)
        x_p, h_p = channel_mix(rows_p, x_p, [oa, ob, oc])

        qa, ka, va, qb, kb, vb, uc, vc = mixer_inputs(rows_s, h_s)
        r3 = lambda a, heads: a[:dec_batch].reshape(dec_batch, heads, LANES)
        oa = _moba_decode(page_table, r3(qa, h_a), r3(ka, h_a), r3(va, h_a), ck_a, cv_a, l)
        ob = _diff_decode(page_table, r3(qb, h_b), r3(kb, h_b), r3(vb, h_b), cache_b_k, cache_b_v, lamp, diff_gain,
                          l, lam_init, dk_b)
        oc = _gmlp_sample(uc, vc, w_s, b_st, l)
        pad_rows = lambda a: jnp.pad(a.reshape(dec_batch, -1), ((0, n_s - dec_batch), (0, 0)))
        x_s, h_s = channel_mix(rows_s, x_s, [pad_rows(oa), pad_rows(ob), oc])
        outs_s.append((ka[:dec_batch], va[:dec_batch], kb[:dec_batch], vb[:dec_batch], vc[:dec_batch]))

    stack = lambda rows, i: jnp.stack([r[i] for r in rows], axis=0)
    return (x_p.reshape(batch, seq, d),
            x_s[:dec_batch].reshape(dec_batch, 1, d),
            jnp.swapaxes(caches_p[0], 2, 3),
            jnp.swapaxes(caches_p[1], 2, 3),
            caches_p[2].reshape(depth, batch, seq, h_b, 2 * dk_b),
            caches_p[3].reshape(depth, batch, seq, h_b, dv_b),
            stack(outs_s, 0).reshape(depth, dec_batch, 1, h_a, dh_a),
            stack(outs_s, 1).reshape(depth, dec_batch, 1, h_a, dh_a),
            stack(outs_s, 2).reshape(depth, dec_batch, 1, h_b, 2 * dk_b),
            stack(outs_s, 3).reshape(depth, dec_batch, 1, h_b, dv_b),
            stack(outs_s, 4).reshape(depth, dec_batch, 1, w_c))
```

```python
import functools
import math

import jax
import jax.numpy as jnp
from jax import lax
from jax.experimental import pallas as pl
from jax.experimental.pallas import tpu as pltpu

F32 = jnp.float32
BF16 = jnp.bfloat16
HIGHEST = lax.Precision.HIGHEST
NEG_INF = float("-inf")

LANES = 128
MOBA_BLOCK = 256
MOBA_TOPK = 3
PEER_TOPK = 16
ROPE_THETA = 500000.0
LN_EPS = 1e-5
SAMPLE_ROWS = 128
VMEM_MB = 1024 * 1024


def _params(n_axes, vmem_mb):
    return pltpu.CompilerParams(dimension_semantics=("arbitrary",) * n_axes,
                                vmem_limit_bytes=vmem_mb * VMEM_MB)


def _tile(n, preferred):
    t = min(n, preferred) // LANES * LANES
    while n % t:
        t -= LANES
    return t


def _nt(a, b, **kw):
    return lax.dot_general(a, b, (((1,), (1,)), ((), ())), preferred_element_type=F32, **kw)


def _ada_kernel(c_ref, w_ref, b_ref, o_ref):
    a = jax.nn.silu(c_ref[...]).astype(BF16)
    o_ref[...] = jnp.dot(a, w_ref[...].astype(BF16), preferred_element_type=F32) + b_ref[...]


def _ada_mods(c_all, ada_w, ada_b):
    depth, d, n = ada_w.shape
    rows = c_all.shape[0]
    tn = 512
    return pl.pallas_call(
        _ada_kernel, grid=(depth, n // tn),
        in_specs=[pl.BlockSpec((rows, d), lambda l, j: (0, 0)),
                  pl.BlockSpec((None, d, tn), lambda l, j: (l, 0, j)),
                  pl.BlockSpec((None, 1, tn), lambda l, j: (l, 0, j))],
        out_specs=pl.BlockSpec((None, rows, tn), lambda l, j: (l, 0, j)),
        out_shape=jax.ShapeDtypeStruct((depth, rows, n), F32),
        compiler_params=_params(2, 40), name="ada_mods",
    )(c_all, ada_w, ada_b.reshape(depth, 1, n))


class _Rows:
    def __init__(self, n_rows, tm, mods, rope_tab, seq):
        self.n_rows, self.tm, self.mods, self.rope_tab, self.seq = n_rows, tm, mods, rope_tab, seq
        self.per_row = mods.ndim == 4
        self.d = mods.shape[-1]

    def mod_spec(self, layer, k, tm=None):
        tm = tm or self.tm
        if self.per_row:
            return pl.BlockSpec((None, None, tm, self.d), lambda i: (layer, k, i, 0))
        seq = self.seq
        return pl.BlockSpec((None, None, None, 1, self.d), lambda i: (layer, (i * tm) // seq, k, 0, 0))

    def rope_spec(self, tm=None):
        tm = tm or self.tm
        w = self.rope_tab.shape[1]
        if self.per_row:
            return pl.BlockSpec((tm, w), lambda i: (i, 0))
        per_seq = self.seq // tm
        return pl.BlockSpec((tm, w), lambda i: (i % per_seq, 0))


def _modulate_kernel(x_ref, sc_ref, sh_ref, h_ref):
    h_ref[...] = (x_ref[...] * (1.0 + sc_ref[...]) + sh_ref[...]).astype(BF16)


def _modulate(rows, x, layer):
    tm, d = rows.tm, rows.d
    return pl.pallas_call(
        _modulate_kernel, grid=(rows.n_rows // tm,),
        in_specs=[pl.BlockSpec((tm, d), lambda i: (i, 0)), rows.mod_spec(layer, 1), rows.mod_spec(layer, 0)],
        out_specs=pl.BlockSpec((tm, d), lambda i: (i, 0)),
        out_shape=jax.ShapeDtypeStruct((rows.n_rows, d), BF16),
        compiler_params=_params(1, 32), name="modulate",
    )(x, rows.mods, rows.mods)


def _mm_kernel(*refs, ksizes):
    b_ref, o_ref = refs[len(ksizes)], refs[-1]
    acc, off = None, 0
    for a_ref, k in zip(refs, ksizes):
        part = jnp.dot(a_ref[...], b_ref[off:off + k, :], preferred_element_type=F32)
        acc = part if acc is None else acc + part
        off += k
    o_ref[...] = acc.astype(o_ref.dtype)


def _mm(a_list, b, layer, tm, tn, out_dtype, vmem_mb, name):
    m = a_list[0].shape[0]
    ks = tuple(a.shape[1] for a in a_list)
    k_all, n = b.shape[1], b.shape[2]
    assert sum(ks) == k_all and m % tm == 0 and n % tn == 0
    in_specs = [pl.BlockSpec((tm, k), lambda i, j: (i, 0)) for k in ks]
    in_specs.append(pl.BlockSpec((None, k_all, tn), lambda i, j: (layer, 0, j)))
    return pl.pallas_call(
        functools.partial(_mm_kernel, ksizes=ks), grid=(m // tm, n // tn), in_specs=in_specs,
        out_specs=pl.BlockSpec((tm, tn), lambda i, j: (i, j)),
        out_shape=jax.ShapeDtypeStruct((m, n), out_dtype),
        compiler_params=_params(2, vmem_mb), name=name,
    )(*a_list, b)


def _rope_tables(pos, dh_a, dk_b):
    lane = jnp.arange(LANES)

    def tabs(dh):
        r, half = dh // 4, dh // 8
        inv = ROPE_THETA ** (-jnp.arange(half, dtype=F32) * (2.0 / r))
        ang = pos.astype(F32)[:, None] * inv[None, :]
        cos, sin = jnp.cos(ang), jnp.sin(ang)
        ld = lane % dh
        idx = ld % half
        c = jnp.where(ld < r, cos[:, idx], 1.0)
        s_up = jnp.where((ld >= half) & (ld < r), sin[:, idx], 0.0)
        s_dn = jnp.where(ld < half, -sin[:, idx], 0.0)
        return [c, s_up, s_dn]

    return jnp.concatenate(tabs(dh_a) + tabs(dk_b), axis=1).astype(F32)


def _post_kernel(*refs, w_a, w_b, w_c, half_a, half_b, n_prev, stacked):
    p_ref, tab_ref, lcg_ref, lcb_ref = refs[:4]
    qa_ref, ka_ref, va_ref, qb_ref, kb_ref, vb_ref, uc_ref, vc_ref = refs[4 + n_prev:12 + n_prev]
    cache_refs = refs[12 + n_prev:] if stacked else (None,) * 4
    tab = tab_ref[...]
    t = [tab[:, i * LANES:(i + 1) * LANES] for i in range(6)]

    def rope(x, c, s_up, s_dn, half):
        return x * c + pltpu.roll(x, half, 1) * s_up + pltpu.roll(x, LANES - half, 1) * s_dn

    def split_cols(src_off, dst_ref, width, tabs=None, half=None, slab_ref=None, head_major=False):
        for j in range(width // LANES):
            x = p_ref[:, src_off + j * LANES: src_off + (j + 1) * LANES]
            if tabs is not None:
                x = rope(x, *tabs, half)
            dst_ref[:, j * LANES:(j + 1) * LANES] = x.astype(dst_ref.dtype)
            if slab_ref is not None and head_major:
                slab_ref[j] = x
            elif slab_ref is not None:
                slab_ref[:, j, :] = x

    off = 0
    split_cols(off, qa_ref, w_a, t[0:3], half_a); off += w_a
    split_cols(off, ka_ref, w_a, t[0:3], half_a, cache_refs[0], True); off += w_a
    split_cols(off, va_ref, w_a, slab_ref=cache_refs[1], head_major=True); off += w_a
    split_cols(off, qb_ref, w_b, t[3:6], half_b); off += w_b
    split_cols(off, kb_ref, w_b, t[3:6], half_b, cache_refs[2]); off += w_b
    split_cols(off, vb_ref, w_b, slab_ref=cache_refs[3]); off += w_b
    uc_ref[...] = jax.nn.gelu(p_ref[:, off:off + w_c]); off += w_c
    g = jax.nn.gelu(p_ref[:, off:off + w_c])
    mu = jnp.mean(g, axis=-1, keepdims=True)
    gc = g - mu
    var = jnp.mean(gc * gc, axis=-1, keepdims=True)
    vc_ref[...] = gc * lax.rsqrt(var + LN_EPS) * lcg_ref[...] + lcb_ref[...]


def _post(rows, proj, ln_c_g, ln_c_b, layer, w_a, w_b, w_c, dh_a, dk_b, stacked=False, prev=()):
    tm, n = rows.tm, rows.n_rows
    depth = ln_c_g.shape[0]
    widths = (w_a, w_a, w_a, w_b, w_b, w_b, w_c, w_c)
    kv = BF16 if stacked else F32
    dtypes = (BF16, kv, kv, BF16, kv, kv, F32, F32)
    row_spec = lambda w: pl.BlockSpec((tm, w), lambda i: (i, 0))
    in_specs = [row_spec(proj.shape[1]), rows.rope_spec(),
                pl.BlockSpec((None, 1, w_c), lambda i: (layer, 0, 0)),
                pl.BlockSpec((None, 1, w_c), lambda i: (layer, 0, 0))]
    out_specs = [row_spec(w) for w in widths]
    out_shape = [jax.ShapeDtypeStruct((n, w), dt) for w, dt in zip(widths, dtypes)]
    aliases = {}
    if stacked:
        seq, per_seq = rows.seq, rows.seq // tm
        for w in (w_a, w_a):
            heads = w // LANES
            out_specs.append(pl.BlockSpec((None, None, heads, tm, LANES),
                                          lambda i: (layer, i // per_seq, 0, i % per_seq, 0)))
            out_shape.append(jax.ShapeDtypeStruct((depth, n // seq, heads, seq, LANES), F32))
        for w in (w_b, w_b):
            heads = w // LANES
            out_specs.append(pl.BlockSpec((None, tm, heads, LANES), lambda i: (layer, i, 0, 0)))
            out_shape.append(jax.ShapeDtypeStruct((depth, n, heads, LANES), F32))
        in_specs += [pl.BlockSpec(memory_space=pl.ANY)] * len(prev)
        aliases = {4 + k: 8 + k for k in range(len(prev))}
    return pl.pallas_call(
        functools.partial(_post_kernel, w_a=w_a, w_b=w_b, w_c=w_c, half_a=dh_a // 8, half_b=dk_b // 8,
                          n_prev=len(prev), stacked=stacked),
        grid=(n // tm,), in_specs=in_specs, out_specs=out_specs, out_shape=out_shape,
        input_output_aliases=aliases,
        compiler_params=_params(1, 48), name="proj_split",
    )(proj, rows.rope_tab, ln_c_g.reshape(depth, 1, w_c), ln_c_b.reshape(depth, 1, w_c), *prev)


def _heads_per_step(heads, most=4):
    return max(h for h in range(1, most + 1) if heads % h == 0)


def _moba_kernel(q_ref, k_ref, v_ref, o_ref, km_ref, *, nb, scale, hp):
    blk = MOBA_BLOCK
    qi = pl.program_id(2)

    @pl.when(qi == 0)
    def _():
        km_ref[...] = jnp.zeros_like(km_ref)
        for h in range(hp):
            for j in range(nb):
                kb = k_ref[j * blk:(j + 1) * blk, h * LANES:(h + 1) * LANES].astype(F32)
                km_ref[h, j:j + 1, :] = jnp.mean(kb, axis=0, keepdims=True)

    lane = lax.broadcasted_iota(jnp.int32, (blk, LANES), 1)
    row = lax.broadcasted_iota(jnp.int32, (blk, blk), 0)
    col = lax.broadcasted_iota(jnp.int32, (blk, blk), 1)

    def block(h, q, j):
        start = pl.multiple_of(j * blk, blk)
        kj = k_ref[pl.ds(start, blk), h * LANES:(h + 1) * LANES].astype(BF16)
        vj = v_ref[pl.ds(start, blk), h * LANES:(h + 1) * LANES].astype(BF16)
        return _nt(q, kj) * scale, vj

    qs, sels, states = [], [], []
    for h in range(hp):
        q = q_ref[:, h * LANES:(h + 1) * LANES]
        gate = _nt(q.astype(F32), km_ref[h], precision=HIGHEST)
        g = jnp.where(lane < qi, gate, NEG_INF)
        rank = jnp.zeros((blk, LANES), F32)
        for m in range(nb):
            gm = g[:, m:m + 1]
            tie_first = jnp.where(lane > m, 1.0, 0.0)
            rank = rank + jnp.where(gm > g, 1.0, jnp.where(gm == g, tie_first, 0.0))
        sels.append(jnp.where(lane < qi, jnp.where(rank < MOBA_TOPK, 1.0, 0.0), 0.0))
        s, vj = block(h, q, qi)
        s = jnp.where(col <= row, s, NEG_INF)
        m0 = jnp.max(s, axis=1, keepdims=True)
        p = jnp.exp(s - m0)
        states.append((m0, jnp.sum(p, axis=1, keepdims=True), jnp.dot(p.astype(BF16), vj, preferred_element_type=F32)))
        qs.append(q)

    def body(j, carry):
        out = []
        for h, (m_run, l_run, acc) in enumerate(carry):
            s, vj = block(h, qs[h], j)
            chosen = jnp.max(jnp.where(lane == j, sels[h], 0.0), axis=1, keepdims=True)
            s = jnp.where(chosen > 0.0, s, NEG_INF)
            m_new = jnp.maximum(m_run, jnp.max(s, axis=1, keepdims=True))
            alpha = jnp.exp(m_run - m_new)
            p = jnp.exp(s - m_new)
            out.append((m_new, alpha * l_run + jnp.sum(p, axis=1, keepdims=True),
                        alpha * acc + jnp.dot(p.astype(BF16), vj, preferred_element_type=F32)))
        return tuple(out)

    final = lax.fori_loop(0, qi, body, tuple(states))
    for h, (_, l_fin, acc) in enumerate(final):
        o_ref[:, h * LANES:(h + 1) * LANES] = (acc / l_fin).astype(o_ref.dtype)


def _moba_prompt(q, k, v, batch, seq, heads):
    nb = seq // MOBA_BLOCK
    hp = _heads_per_step(heads)
    assert seq % MOBA_BLOCK == 0 and nb <= LANES
    return pl.pallas_call(
        functools.partial(_moba_kernel, nb=nb, scale=LANES ** -0.5, hp=hp),
        grid=(batch, heads // hp, nb),
        in_specs=[pl.BlockSpec((MOBA_BLOCK, hp * LANES), lambda b, g, i: (b * nb + i, g)),
                  pl.BlockSpec((seq, hp * LANES), lambda b, g, i: (b, g)),
                  pl.BlockSpec((seq, hp * LANES), lambda b, g, i: (b, g))],
        out_specs=pl.BlockSpec((MOBA_BLOCK, hp * LANES), lambda b, g, i: (b * nb + i, g)),
        out_shape=jax.ShapeDtypeStruct(q.shape, BF16),
        scratch_shapes=[pltpu.VMEM((hp, LANES, LANES), F32)],
        compiler_params=_params(3, 32), name="moba_prompt",
    )(q, k, v)


def _lam(lamp_ref, lam_init):
    lp = lamp_ref[...]
    d1 = jnp.sum(lp[0:1] * lp[1:2], axis=1, keepdims=True)
    d2 = jnp.sum(lp[2:3] * lp[3:4], axis=1, keepdims=True)
    return jnp.exp(d1) - jnp.exp(d2) + lam_init


def _diff_finish(o, gain, lam_init):
    ms = jnp.mean(o * o, axis=-1, keepdims=True)
    return o * lax.rsqrt(ms + LN_EPS) * gain * (1.0 - lam_init)


def _diff_kernel(lamp_ref, gain_ref, q_ref, k_ref, v_ref, o_ref, *, tq, dk, lam_init, hp):
    qi = pl.program_id(2)
    scale = dk ** -0.5
    lam = _lam(lamp_ref, lam_init)
    lane = lax.broadcasted_iota(jnp.int32, (tq, LANES), 1)
    causal = lax.broadcasted_iota(jnp.int32, (tq, tq), 1) <= lax.broadcasted_iota(jnp.int32, (tq, tq), 0)

    def scores(h, qq, j):
        start = pl.multiple_of(j * tq, tq)
        kj = k_ref[pl.ds(start, tq), h * LANES:(h + 1) * LANES].astype(BF16)
        vj = v_ref[pl.ds(start, tq), h * LANES:(h + 1) * LANES].astype(BF16)
        return _nt(qq[0], kj) * scale, _nt(qq[1], kj) * scale, vj

    def first(s, vj):
        m = jnp.max(s, axis=1, keepdims=True)
        p = jnp.exp(s - m)
        return m, jnp.sum(p, axis=1, keepdims=True), jnp.dot(p.astype(BF16), vj, preferred_element_type=F32)

    def update(state, s, vj):
        m_run, l_run, acc = state
        m_new = jnp.maximum(m_run, jnp.max(s, axis=1, keepdims=True))
        alpha = jnp.exp(m_run - m_new)
        p = jnp.exp(s - m_new)
        return (m_new, alpha * l_run + jnp.sum(p, axis=1, keepdims=True),
                alpha * acc + jnp.dot(p.astype(BF16), vj, preferred_element_type=F32))

    qqs, states = [], []
    for h in range(hp):
        q = q_ref[:, h * LANES:(h + 1) * LANES]
        zero = jnp.zeros_like(q)
        qq = (jnp.where(lane < dk, q, zero), jnp.where(lane >= dk, q, zero))
        s1, s2, vj = scores(h, qq, qi)
        states.append((first(jnp.where(causal, s1, NEG_INF), vj), first(jnp.where(causal, s2, NEG_INF), vj)))
        qqs.append(qq)

    def body(j, carry):
        out = []
        for h, (a, b) in enumerate(carry):
            s1, s2, vj = scores(h, qqs[h], j)
            out.append((update(a, s1, vj), update(b, s2, vj)))
        return tuple(out)

    final = lax.fori_loop(0, qi, body, tuple(states))
    for h, (st1, st2) in enumerate(final):
        o = st1[2] / st1[1] - lam * (st2[2] / st2[1])
        o_ref[:, h * LANES:(h + 1) * LANES] = _diff_finish(o, gain_ref[...], lam_init).astype(o_ref.dtype)


def _diff_prompt(q, k, v, lamp, gain, layer, lam_init, batch, seq, heads, dk):
    tq = 256
    nq = seq // tq
    depth = lamp.shape[0]
    hp = _heads_per_step(heads)
    return pl.pallas_call(
        functools.partial(_diff_kernel, tq=tq, dk=dk, lam_init=lam_init, hp=hp),
        grid=(batch, heads // hp, nq),
        in_specs=[pl.BlockSpec((None, 4, dk), lambda b, g, i: (layer, 0, 0)),
                  pl.BlockSpec((None, 1, LANES), lambda b, g, i: (layer, 0, 0)),
                  pl.BlockSpec((tq, hp * LANES), lambda b, g, i: (b * nq + i, g)),
                  pl.BlockSpec((seq, hp * LANES), lambda b, g, i: (b, g)),
                  pl.BlockSpec((seq, hp * LANES), lambda b, g, i: (b, g))],
        out_specs=pl.BlockSpec((tq, hp * LANES), lambda b, g, i: (b * nq + i, g)),
        out_shape=jax.ShapeDtypeStruct(q.shape, BF16),
        compiler_params=_params(3, 32), name="diff_prompt",
    )(lamp, gain.reshape(depth, 1, LANES), q, k, v)


def _gmlp_kernel(u_ref, v_ref, w_ref, bt_ref, o_ref, *, groups, chunk):
    row = lax.broadcasted_iota(jnp.int32, (chunk, chunk), 0)
    col = lax.broadcasted_iota(jnp.int32, (chunk, chunk), 1)
    causal = col <= row
    for g in range(groups):
        sl = slice(g * LANES, (g + 1) * LANES)
        w = jnp.where(causal, w_ref[g], 0.0).astype(BF16)
        mixed = jnp.dot(w, v_ref[:, sl].astype(BF16), preferred_element_type=F32) + bt_ref[:, g:g + 1]
        o_ref[:, sl] = (u_ref[:, sl] * mixed).astype(o_ref.dtype)


def _gmlp_prompt(u, v, w_s, b_st, layer):
    n, w_c = u.shape
    groups, chunk = w_s.shape[1], w_s.shape[2]
    return pl.pallas_call(
        functools.partial(_gmlp_kernel, groups=groups, chunk=chunk), grid=(n // chunk,),
        in_specs=[pl.BlockSpec((chunk, w_c), lambda i: (i, 0)), pl.BlockSpec((chunk, w_c), lambda i: (i, 0)),
                  pl.BlockSpec((None, groups, chunk, chunk), lambda i: (layer, 0, 0, 0)),
                  pl.BlockSpec((None, chunk, groups), lambda i: (layer, 0, 0))],
        out_specs=pl.BlockSpec((chunk, w_c), lambda i: (i, 0)),
        out_shape=jax.ShapeDtypeStruct((n, w_c), BF16),
        compiler_params=_params(1, 32), name="gmlp_prompt",
    )(u, v, w_s, b_st)


def _gmlp_first_kernel(u_ref, v_ref, w_ref, bt_ref, o_ref, *, groups):
    for g in range(groups):
        sl = slice(g * LANES, (g + 1) * LANES)
        mixed = w_ref[g, 0:1, 0:1] * v_ref[:, sl] + bt_ref[0:1, g:g + 1]
        o_ref[:, sl] = (u_ref[:, sl] * mixed).astype(o_ref.dtype)


def _gmlp_sample(u, v, w_s, b_st, layer):
    n, w_c = u.shape
    groups, chunk = w_s.shape[1], w_s.shape[2]
    return pl.pallas_call(
        functools.partial(_gmlp_first_kernel, groups=groups), grid=(1,),
        in_specs=[pl.BlockSpec((n, w_c), lambda i: (0, 0)), pl.BlockSpec((n, w_c), lambda i: (0, 0)),
                  pl.BlockSpec((None, groups, chunk, chunk), lambda i: (layer, 0, 0, 0)),
                  pl.BlockSpec((None, chunk, groups), lambda i: (layer, 0, 0))],
        out_specs=pl.BlockSpec((n, w_c), lambda i: (0, 0)),
        out_shape=jax.ShapeDtypeStruct((n, w_c), BF16),
        compiler_params=_params(1, 32), name="gmlp_sample",
    )(u, v, w_s, b_st)


def _ln_kernel(*refs, alpha, emit_h, y_transposed, emit_ht):
    x_ref, y_ref, gate_ref, g_ref, b_ref = refs[:5]
    y = y_ref[...].T if y_transposed else y_ref[...]
    z = alpha * x_ref[...] + gate_ref[...] * y
    mu = jnp.mean(z, axis=-1, keepdims=True)
    zc = z - mu
    var = jnp.mean(zc * zc, axis=-1, keepdims=True)
    xn = zc * lax.rsqrt(var + LN_EPS) * g_ref[...] + b_ref[...]
    if emit_h:
        sc_ref, sh_ref, xo_ref, h_ref = refs[5:9]
        h = xn * (1.0 + sc_ref[...]) + sh_ref[...]
        h_ref[...] = h.astype(BF16)
        if emit_ht:
            refs[9][...] = h.T.astype(BF16)
    else:
        xo_ref = refs[5]
    xo_ref[...] = xn


def _ln(rows, x, y, ln_g, ln_b, layer, gate_k, alpha, next_mod=None, y_transposed=False, emit_ht=False):
    tm, d, n = min(rows.tm, 256), rows.d, rows.n_rows
    depth = ln_g.shape[0]
    row_spec = pl.BlockSpec((tm, d), lambda i: (i, 0))
    col_spec = pl.BlockSpec((d, tm), lambda i: (0, i))
    par_spec = pl.BlockSpec((None, 1, d), lambda i: (layer, 0, 0))
    in_specs = [row_spec, col_spec if y_transposed else row_spec, rows.mod_spec(layer, gate_k, tm), par_spec, par_spec]
    args = [x, y, rows.mods, ln_g.reshape(depth, 1, d), ln_b.reshape(depth, 1, d)]
    out_specs, out_shape = [row_spec], [jax.ShapeDtypeStruct((n, d), F32)]
    if next_mod is not None:
        nl, k_sc, k_sh = next_mod
        in_specs += [rows.mod_spec(nl, k_sc, tm), rows.mod_spec(nl, k_sh, tm)]
        args += [rows.mods, rows.mods]
        out_specs.append(row_spec)
        out_shape.append(jax.ShapeDtypeStruct((n, d), BF16))
        if emit_ht:
            out_specs.append(col_spec)
            out_shape.append(jax.ShapeDtypeStruct((d, n), BF16))
    out = pl.pallas_call(
        functools.partial(_ln_kernel, alpha=alpha, emit_h=next_mod is not None, y_transposed=y_transposed,
                          emit_ht=emit_ht),
        grid=(n // tm,), in_specs=in_specs, out_specs=out_specs, out_shape=out_shape,
        compiler_params=_params(1, 48), name="deepnorm_ln",
    )(*args)
    return tuple(out) if next_mod is not None else (out[0], None)


def _peer_select_kernel(q_ref, keys_ref, s1_ref, e1_ref, s2_ref, e2_ref, tau_ref, cand_ref, *, heads, n_keys, pairs):
    tm = q_ref.shape[0]
    sub = lax.broadcasted_iota(jnp.int32, (n_keys, tm), 0)
    csub = lax.broadcasted_iota(jnp.int32, cand_ref.shape, 0)

    def pop_max(x, iota, big):
        m = jnp.max(x, axis=0, keepdims=True)
        first = jnp.min(jnp.where(x == m, iota, big), axis=0, keepdims=True)
        return m, jnp.where(iota == first, NEG_INF, x)

    def top_rows(s):
        rows, x = [], s
        for _ in range(PEER_TOPK):
            m, x = pop_max(x, sub, n_keys)
            rows.append(m)
        return rows

    for h in range(heads):
        s1 = _nt(keys_ref[0], q_ref[:, (2 * h) * n_keys:(2 * h + 1) * n_keys], precision=HIGHEST)
        s2 = _nt(keys_ref[1], q_ref[:, (2 * h + 1) * n_keys:(2 * h + 2) * n_keys], precision=HIGHEST)
        a1, a2 = top_rows(s1), top_rows(s2)
        cand_ref[...] = jnp.full(cand_ref.shape, NEG_INF, F32)
        for r, (i, j) in enumerate(pairs):
            cand_ref[r:r + 1, :] = a1[i] + a2[j]
        c = cand_ref[...]
        x, tau = c, None
        for _ in range(PEER_TOPK):
            tau, x = pop_max(x, csub, cand_ref.shape[0])
        cmax = a1[0] + a2[0]
        z = jnp.sum(jnp.where(c >= tau, jnp.exp(c - cmax), 0.0), axis=0, keepdims=True)
        s1_ref[h] = s1
        s2_ref[h] = s2
        e1_ref[h] = jnp.exp(s1 - a1[0]) / z
        e2_ref[h] = jnp.exp(s2 - a2[0])
        tau_ref[h] = tau


def _peer_select(q, keys, layer, heads, tm):
    n, n_keys = q.shape[0], keys.shape[2]
    pairs = tuple((i, j) for i in range(PEER_TOPK) for j in range(PEER_TOPK) if (i + 1) * (j + 1) <= PEER_TOPK)
    n_cand = -(-len(pairs) // 8) * 8
    big = pl.BlockSpec((heads, n_keys, tm), lambda i: (0, 0, i))
    big_shape = jax.ShapeDtypeStruct((heads, n_keys, n), F32)
    return pl.pallas_call(
        functools.partial(_peer_select_kernel, heads=heads, n_keys=n_keys, pairs=pairs), grid=(n // tm,),
        in_specs=[pl.BlockSpec((tm, q.shape[1]), lambda i: (i, 0)),
                  pl.BlockSpec((None, 2, n_keys, keys.shape[3]), lambda i: (layer, 0, 0, 0))],
        out_specs=[big, big, big, big, pl.BlockSpec((heads, 1, tm), lambda i: (0, 0, i))],
        out_shape=[big_shape] * 4 + [jax.ShapeDtypeStruct((heads, 1, n), F32)],
        scratch_shapes=[pltpu.VMEM((n_cand, tm), F32)],
        compiler_params=_params(1, 32), name="peer_select",
    )(q, keys)


def _peer_dense_kernel(ht_ref, u_ref, vt_ref, s1_ref, e1_ref, s2_ref, e2_ref, tau_ref, o_ref, w_ref, g_ref,
                       *, heads, n_keys, n1_per_step):
    j = pl.program_id(1)
    tm = ht_ref.shape[1]

    @pl.when(j == 0)
    def _():
        o_ref[...] = jnp.zeros_like(o_ref)

    for r0 in range(0, n1_per_step, 2):
        pair = slice(r0 * n_keys, (r0 + 2) * n_keys)
        s1_rows = [[s1_ref[h, pl.ds(j * n1_per_step + r0 + k, 1), :] for k in range(2)] for h in range(heads)]
        e1_rows = [[e1_ref[h, pl.ds(j * n1_per_step + r0 + k, 1), :] for k in range(2)] for h in range(heads)]
        for c in range(tm // LANES):
            cols = slice(c * LANES, (c + 1) * LANES)
            g = [None, None]
            for h in range(heads):
                s2, e2, tau = s2_ref[h, :, cols], e2_ref[h, :, cols], tau_ref[h, :, cols]
                for k in range(2):
                    term = e1_rows[h][k][:, cols] * jnp.where(s1_rows[h][k][:, cols] + s2 >= tau, e2, 0.0)
                    g[k] = term if g[k] is None else g[k] + term
            for k in range(2):
                g_ref[(r0 + k) * n_keys:(r0 + k + 1) * n_keys, cols] = g[k]
        st = jnp.dot(u_ref[pair, :], ht_ref[...], preferred_element_type=F32)
        w_ref[pair, :] = (g_ref[pair, :] * jax.nn.gelu(st)).astype(BF16)
    rows_per_chunk = 512
    for m in range(vt_ref.shape[0] // rows_per_chunk):
        rows = slice(m * rows_per_chunk, (m + 1) * rows_per_chunk)
        o_ref[rows, :] += jnp.dot(vt_ref[rows, :], w_ref[...], preferred_element_type=F32)


def _peer_dense(h2t, u, vt, sel, layer, heads, tm, n1_per_step, vmem_mb):
    d, n = h2t.shape
    n_keys = sel[0].shape[1]
    et = n1_per_step * n_keys
    n_steps = n_keys * n_keys // et
    assert n1_per_step % 2 == 0 and n_keys % n1_per_step == 0
    once = pl.Buffered(1)
    resident = pl.BlockSpec((heads, n_keys, tm), lambda i, j: (0, 0, i), pipeline_mode=once)
    return pl.pallas_call(
        functools.partial(_peer_dense_kernel, heads=heads, n_keys=n_keys, n1_per_step=n1_per_step),
        grid=(n // tm, n_steps),
        in_specs=[pl.BlockSpec((d, tm), lambda i, j: (0, i), pipeline_mode=once),
                  pl.BlockSpec((None, et, d), lambda i, j: (layer, j, 0)),
                  pl.BlockSpec((None, d, et), lambda i, j: (layer, 0, j)),
                  resident, resident, resident, resident,
                  pl.BlockSpec((heads, 1, tm), lambda i, j: (0, 0, i))],
        out_specs=pl.BlockSpec((d, tm), lambda i, j: (0, i)),
        out_shape=jax.ShapeDtypeStruct((d, n), F32),
        scratch_shapes=[pltpu.VMEM((et, tm), BF16), pltpu.VMEM((et, tm), F32)],
        compiler_params=_params(2, vmem_mb), name="peer_dense",
    )(h2t, u, vt, *sel)


def _moba_decode_kernel(pt_ref, q_ref, kn_ref, vn_ref, *refs, pages_per_block, n_blocks, scale):
    del pt_ref
    k_refs, v_refs = refs[:pages_per_block], refs[pages_per_block:2 * pages_per_block]
    o_ref, ks_ref, m_ref, l_ref, acc_ref = refs[2 * pages_per_block:]
    blk = pl.program_id(1)
    heads, page = k_refs[0].shape[0], k_refs[0].shape[1]
    qf = q_ref[...].astype(F32)

    s_pages, ksum = [], None
    for k_ref in k_refs:
        s_rows, ksum_rows = [], []
        for h in range(heads):
            kh = k_ref[h]
            q16 = jnp.broadcast_to(qf[h:h + 1, :], (16, LANES)).astype(BF16)
            s_rows.append(_nt(q16, kh.astype(BF16))[0:1, :])
            ksum_rows.append(jnp.sum(kh, axis=0, keepdims=True))
        s_pages.append(jnp.concatenate(s_rows, axis=0) * scale)
        part = jnp.concatenate(ksum_rows, axis=0)
        ksum = part if ksum is None else ksum + part
    m_blk = functools.reduce(jnp.maximum, [jnp.max(s, axis=1, keepdims=True) for s in s_pages])
    m_blk = jnp.broadcast_to(m_blk, qf.shape)
    l_blk, acc = None, None
    for s, v_ref in zip(s_pages, v_refs):
        pe = jnp.exp(s - m_blk)
        pv = jnp.concatenate(
            [jnp.dot(jnp.broadcast_to(pe[h:h + 1, :], (16, page)).astype(BF16), v_ref[h].astype(BF16),
                     preferred_element_type=F32)[0:1, :] for h in range(heads)], axis=0)
        l_part = jnp.sum(pe, axis=1, keepdims=True)
        l_blk = l_part if l_blk is None else l_blk + l_part
        acc = pv if acc is None else acc + pv
    m_ref[blk] = m_blk
    l_ref[blk] = jnp.broadcast_to(l_blk, qf.shape)
    acc_ref[blk] = acc
    ks_ref[blk] = ksum

    @pl.when(blk == n_blocks - 1)
    def _():
        shape = ks_ref.shape
        nb = shape[0]
        gate = jnp.sum(ks_ref[...] * (1.0 / MOBA_BLOCK) * qf, axis=-1, keepdims=True)
        g = jnp.broadcast_to(gate, shape)
        bi = lax.broadcasted_iota(jnp.int32, shape, 0)
        chosen = jnp.zeros(shape, F32)
        for _ in range(MOBA_TOPK):
            gmax = jnp.max(g, axis=0)
            first = jnp.min(jnp.where(g == gmax, bi, nb), axis=0)
            hit = bi == first
            chosen = jnp.where(hit, 1.0, chosen)
            g = jnp.where(hit, NEG_INF, g)
        s_own = jnp.broadcast_to(jnp.sum(qf * kn_ref[...], axis=-1, keepdims=True) * scale, qf.shape)
        m_all = m_ref[...]
        m_fin = jnp.maximum(s_own, jnp.max(jnp.where(chosen > 0.0, m_all, NEG_INF), axis=0))
        wts = jnp.where(chosen > 0.0, jnp.exp(m_all - m_fin), 0.0)
        w_own = jnp.exp(s_own - m_fin)
        l_fin = jnp.sum(wts * l_ref[...], axis=0) + w_own
        o = jnp.sum(wts * acc_ref[...], axis=0) + w_own * vn_ref[...]
        o_ref[...] = (o / l_fin).astype(o_ref.dtype)


def _moba_decode(page_table, q3, kn3, vn3, cache_k, cache_v, layer):
    batch, n_pages = page_table.shape
    heads, page = cache_k.shape[2], cache_k.shape[3]
    ppb = MOBA_BLOCK // page
    nb = n_pages // ppb
    assert n_pages % ppb == 0 and nb >= MOBA_TOPK and page == LANES
    row = pl.BlockSpec((None, heads, LANES), lambda b, j, pt: (b, 0, 0))
    page_specs = [pl.BlockSpec((None, None, heads, page, LANES),
                               lambda b, j, pt, i=i: (layer, pt[b, j * ppb + i], 0, 0, 0)) for i in range(ppb)]
    return pl.pallas_call(
        functools.partial(_moba_decode_kernel, pages_per_block=ppb, n_blocks=nb, scale=LANES ** -0.5),
        grid_spec=pltpu.PrefetchScalarGridSpec(
            num_scalar_prefetch=1, grid=(batch, nb),
            in_specs=[row, row, row] + page_specs + page_specs,
            out_specs=row,
            scratch_shapes=[pltpu.VMEM((nb, heads, LANES), F32)] * 4),
        out_shape=jax.ShapeDtypeStruct((batch, heads, LANES), BF16),
        compiler_params=_params(2, 32), name="moba_decode",
    )(page_table, q3, kn3, vn3, *([cache_k] * ppb), *([cache_v] * ppb))


def _diff_decode_kernel(pt_ref, lamp_ref, gain_ref, q_ref, kn_ref, vn_ref, *refs, dk, n_steps, pages_per_step, lam_init):
    del pt_ref
    k_refs, v_refs = refs[:pages_per_step], refs[pages_per_step:2 * pages_per_step]
    o_ref, m_ref, l_ref, acc_ref = refs[2 * pages_per_step:]
    p = pl.program_id(1)
    scale = dk ** -0.5
    qf = q_ref[...].astype(F32)
    heads = qf.shape[0]
    spread = jnp.where(lax.broadcasted_iota(jnp.int32, (LANES, 2 * LANES), 0) // dk
                       == lax.broadcasted_iota(jnp.int32, (LANES, 2 * LANES), 1) // LANES, 1.0, 0.0).astype(BF16)

    def map_scores(prod):
        hi = prod.astype(BF16)
        lo = (prod - hi.astype(F32)).astype(BF16)
        s = (jnp.dot(hi, spread, preferred_element_type=F32) + jnp.dot(lo, spread, preferred_element_type=F32)) * scale
        return s[:, :LANES], s[:, LANES:]

    @pl.when(p == 0)
    def _():
        s_new = map_scores(kn_ref[...] * qf)
        for i in range(2):
            m_ref[i] = s_new[i]
            l_ref[i] = jnp.ones_like(qf)
            acc_ref[i] = vn_ref[...]

    page = k_refs[0].shape[0]
    scores = [map_scores((k_ref[...] * qf).reshape(page * heads, LANES)) for k_ref in k_refs]
    for i in range(2):
        ss = [sc[i].reshape(page, heads, LANES) for sc in scores]
        m_old = m_ref[i]
        m_new = functools.reduce(jnp.maximum, [m_old] + [jnp.max(s, axis=0) for s in ss])
        alpha = jnp.exp(m_old - m_new)
        l_new, acc_new = alpha * l_ref[i], alpha * acc_ref[i]
        for s, v_ref in zip(ss, v_refs):
            pe = jnp.exp(s - m_new)
            l_new = l_new + jnp.sum(pe, axis=0)
            acc_new = acc_new + jnp.sum(pe * v_ref[...], axis=0)
        m_ref[i] = m_new
        l_ref[i] = l_new
        acc_ref[i] = acc_new

    @pl.when(p == n_steps - 1)
    def _():
        lam = _lam(lamp_ref, lam_init)
        o = acc_ref[0] / l_ref[0] - lam * (acc_ref[1] / l_ref[1])
        o_ref[...] = _diff_finish(o, gain_ref[...], lam_init).astype(o_ref.dtype)


def _diff_decode(page_table, q3, kn3, vn3, cache_k, cache_v, lamp, gain, layer, lam_init, dk):
    batch, n_pages = page_table.shape
    page, heads = cache_k.shape[2], cache_k.shape[3]
    depth = lamp.shape[0]
    pps = 2 if n_pages % 2 == 0 else 1
    row = pl.BlockSpec((None, heads, LANES), lambda b, p, pt: (b, 0, 0))
    page_specs = [pl.BlockSpec((None, None, page, heads, LANES),
                               lambda b, p, pt, i=i: (layer, pt[b, p * pps + i], 0, 0, 0)) for i in range(pps)]
    return pl.pallas_call(
        functools.partial(_diff_decode_kernel, dk=dk, n_steps=n_pages // pps, pages_per_step=pps, lam_init=lam_init),
        grid_spec=pltpu.PrefetchScalarGridSpec(
            num_scalar_prefetch=1, grid=(batch, n_pages // pps),
            in_specs=[pl.BlockSpec((None, 4, dk), lambda b, p, pt: (layer, 0, 0)),
                      pl.BlockSpec((None, 1, LANES), lambda b, p, pt: (layer, 0, 0)),
                      row, row, row] + page_specs + page_specs,
            out_specs=row,
            scratch_shapes=[pltpu.VMEM((2, heads, LANES), F32)] * 3),
        out_shape=jax.ShapeDtypeStruct((batch, heads, LANES), BF16),
        compiler_params=_params(2, 32), name="diff_decode",
    )(page_table, lamp, gain.reshape(depth, 1, LANES), q3, kn3, vn3, *([cache_k] * pps), *([cache_v] * pps))


def kernel(x_prompt, x_sample, cache_a_k, cache_a_v, cache_b_k, cache_b_v, page_table, c_prompt, c_sample, ada_w, ada_b, w_in, w_o, lam_q1, lam_k1, lam_q2, lam_k2, diff_gain, ln_c_g, ln_c_b, w_s, b_s, ln1_g, ln1_b, ln2_g, ln2_b, peer_wq, peer_subkeys, peer_u, peer_v):
    batch, seq, d = x_prompt.shape
    dec_batch, dec_seq, _ = x_sample.shape
    depth = w_in.shape[0]
    n_phys, page, h_a, dh_a = cache_a_k.shape[1:]
    h_b, dv_b = cache_b_v.shape[3:]
    dk_b = cache_b_k.shape[4] // 2
    w_a, w_b, w_c = h_a * dh_a, h_b * dv_b, ln_c_g.shape[1]
    n_keys = peer_subkeys.shape[2]
    peer_heads = peer_wq.shape[2] // (2 * peer_subkeys.shape[3])
    past_len = page_table.shape[1] * page
    assert dec_seq == 1 and dh_a == LANES and dv_b == LANES and n_keys == LANES and peer_subkeys.shape[3] == LANES
    assert past_len % MOBA_BLOCK == 0 and w_s.shape[2] == LANES and dec_batch <= SAMPLE_ROWS
    alpha = (2 * depth) ** 0.25
    n_p, n_s = batch * seq, SAMPLE_ROWS

    w_in_b, w_o_b, wq_b = w_in.astype(BF16), w_o.astype(BF16), peer_wq.astype(BF16)
    u_b, vt_b = peer_u.astype(BF16), jnp.swapaxes(peer_v, 1, 2).astype(BF16)
    lamp = jnp.stack([lam_q1, lam_k1, lam_q2, lam_k2], axis=1)
    b_st = jnp.swapaxes(b_s, 1, 2)
    ck_a, cv_a = jnp.swapaxes(cache_a_k, 2, 3), jnp.swapaxes(cache_a_v, 2, 3)

    c_rows = -(-(batch + dec_batch) // 16) * 16
    c_all = jnp.zeros((c_rows, d), F32).at[:batch].set(c_prompt).at[batch:batch + dec_batch].set(c_sample)
    mods = _ada_mods(c_all, ada_w, ada_b)
    mods_p = mods[:, :batch].reshape(depth, batch, 6, 1, d)
    mods_s = mods[:, batch:batch + dec_batch].reshape(depth, dec_batch, 6, d).transpose(0, 2, 1, 3)
    mods_s = jnp.pad(mods_s, ((0, 0), (0, 0), (0, n_s - dec_batch), (0, 0)))

    rows_p = _Rows(n_p, 128, mods_p, _rope_tables(jnp.arange(seq), dh_a, dk_b), seq)
    rows_s = _Rows(n_s, n_s, mods_s, _rope_tables(jnp.full((n_s,), past_len), dh_a, dk_b), 1)

    x_p = x_prompt.reshape(n_p, d)
    x_s = jnp.pad(x_sample.reshape(dec_batch, d), ((0, n_s - dec_batch), (0, 0)))
    h_p = _modulate(rows_p, x_p, 0)
    h_s = _modulate(rows_s, x_s, 0)

    caches_p, outs_s = (), []
    for l in range(depth):
        lam_init = 0.8 - 0.6 * math.exp(-0.3 * l)
        nxt = (l + 1, 1, 0) if l + 1 < depth else None

        def mixer_inputs(rows, h, **cache_args):
            tm = _tile(rows.n_rows, 1024)
            proj = _mm([h], w_in_b, l, tm, _tile(w_in.shape[2], 768), F32, 48, "in_proj")
            return _post(rows, proj, ln_c_g, ln_c_b, l, w_a, w_b, w_c, dh_a, dk_b, **cache_args)

        def channel_mix(rows, x, mix_parts):
            tm = _tile(rows.n_rows, 1024)
            mix = _mm(mix_parts, w_o_b, l, tm, _tile(d, 512), F32, 48, "out_proj")
            x1, h2, h2_t = _ln(rows, x, mix, ln1_g, ln1_b, l, 2, alpha, (l, 4, 3), emit_ht=True)
            pq = _mm([h2], wq_b, l, tm, 512, F32, 48, "peer_query")
            sel = _peer_select(pq, peer_subkeys, l, peer_heads, _tile(rows.n_rows, 256))
            ffn_t = _peer_dense(h2_t, u_b, vt_b, sel, l, peer_heads, _tile(rows.n_rows, 512), 4, 56)
            return _ln(rows, x1, ffn_t, ln2_g, ln2_b, l, 5, alpha, nxt, y_transposed=True)

        qa, ka, va, qb, kb, vb, uc, vc, *caches_p = mixer_inputs(rows_p, h_p, stacked=True, prev=tuple(caches_p))
        oa = _moba_prompt(qa, ka, va, batch, seq, h_a)
        ob = _diff_prompt(qb, kb, vb, lamp, diff_gain, l, lam_init, batch, seq, h_b, dk_b)
        oc = _gmlp_prompt(uc, vc, w_s, b_st, l)
        x_p, h_p = channel_mix(rows_p, x_p, [oa, ob, oc])

        qa, ka, va, qb, kb, vb, uc, vc = mixer_inputs(rows_s, h_s)
        r3 = lambda a, heads: a[:dec_batch].reshape(dec_batch, heads, LANES)
        oa = _moba_decode(page_table, r3(qa, h_a), r3(ka, h_a), r3(va, h_a), ck_a, cv_a, l)
        ob = _diff_decode(page_table, r3(qb, h_b), r3(kb, h_b), r3(vb, h_b), cache_b_k, cache_b_v, lamp, diff_gain,
                          l, lam_init, dk_b)
        oc = _gmlp_sample(uc, vc, w_s, b_st, l)
        pad_rows = lambda a: jnp.pad(a.reshape(dec_batch, -1), ((0, n_s - dec_batch), (0, 0)))
        x_s, h_s = channel_mix(rows_s, x_s, [pad_rows(oa), pad_rows(ob), oc])
        outs_s.append((ka[:dec_batch], va[:dec_batch], kb[:dec_batch], vb[:dec_batch], vc[:dec_batch]))

    stack = lambda rows, i: jnp.stack([r[i] for r in rows], axis=0)
    return (x_p.reshape(batch, seq, d),
            x_s[:dec_batch].reshape(dec_batch, 1, d),
            jnp.swapaxes(caches_p[0], 2, 3),
            jnp.swapaxes(caches_p[1], 2, 3),
            caches_p[2].reshape(depth, batch, seq, h_b, 2 * dk_b),
            caches_p[3].reshape(depth, batch, seq, h_b, dv_b),
            stack(outs_s, 0).reshape(depth, dec_batch, 1, h_a, dh_a),
            stack(outs_s, 1).reshape(depth, dec_batch, 1, h_a, dh_a),
            stack(outs_s, 2).reshape(depth, dec_batch, 1, h_b, 2 * dk_b),
            stack(outs_s, 3).reshape(depth, dec_batch, 1, h_b, dv_b),
            stack(outs_s, 4).reshape(depth, dec_batch, 1, w_c))
```

```python
import functools
import math

import jax
import jax.numpy as jnp
from jax import lax
from jax.experimental import pallas as pl
from jax.experimental.pallas import tpu as pltpu

F32 = jnp.float32
BF16 = jnp.bfloat16
HIGHEST = lax.Precision.HIGHEST
NEG_INF = float("-inf")

LANES = 128
MOBA_BLOCK = 256
MOBA_TOPK = 3
PEER_TOPK = 16
ROPE_THETA = 500000.0
LN_EPS = 1e-5
SAMPLE_ROWS = 128
VMEM_MB = 1024 * 1024


def _params(n_axes, vmem_mb):
    return pltpu.CompilerParams(dimension_semantics=("arbitrary",) * n_axes,
                                vmem_limit_bytes=vmem_mb * VMEM_MB)


def _tile(n, preferred):
    t = min(n, preferred) // LANES * LANES
    while n % t:
        t -= LANES
    return t


def _nt(a, b, **kw):
    return lax.dot_general(a, b, (((1,), (1,)), ((), ())), preferred_element_type=F32, **kw)


def _ada_kernel(c_ref, w_ref, b_ref, o_ref):
    a = jax.nn.silu(c_ref[...]).astype(BF16)
    o_ref[...] = jnp.dot(a, w_ref[...].astype(BF16), preferred_element_type=F32) + b_ref[...]


def _ada_mods(c_all, ada_w, ada_b):
    depth, d, n = ada_w.shape
    rows = c_all.shape[0]
    tn = 512
    return pl.pallas_call(
        _ada_kernel, grid=(depth, n // tn),
        in_specs=[pl.BlockSpec((rows, d), lambda l, j: (0, 0)),
                  pl.BlockSpec((None, d, tn), lambda l, j: (l, 0, j)),
                  pl.BlockSpec((None, 1, tn), lambda l, j: (l, 0, j))],
        out_specs=pl.BlockSpec((None, rows, tn), lambda l, j: (l, 0, j)),
        out_shape=jax.ShapeDtypeStruct((depth, rows, n), F32),
        compiler_params=_params(2, 40), name="ada_mods",
    )(c_all, ada_w, ada_b.reshape(depth, 1, n))


class _Rows:
    def __init__(self, n_rows, tm, mods, rope_tab, seq):
        self.n_rows, self.tm, self.mods, self.rope_tab, self.seq = n_rows, tm, mods, rope_tab, seq
        self.per_row = mods.ndim == 4
        self.d = mods.shape[-1]

    def mod_spec(self, layer, k, tm=None):
        tm = tm or self.tm
        if self.per_row:
            return pl.BlockSpec((None, None, tm, self.d), lambda i: (layer, k, i, 0))
        seq = self.seq
        return pl.BlockSpec((None, None, None, 1, self.d), lambda i: (layer, (i * tm) // seq, k, 0, 0))

    def rope_spec(self, tm=None):
        tm = tm or self.tm
        w = self.rope_tab.shape[1]
        if self.per_row:
            return pl.BlockSpec((tm, w), lambda i: (i, 0))
        per_seq = self.seq // tm
        return pl.BlockSpec((tm, w), lambda i: (i % per_seq, 0))


def _modulate_kernel(x_ref, sc_ref, sh_ref, h_ref):
    h_ref[...] = (x_ref[...] * (1.0 + sc_ref[...]) + sh_ref[...]).astype(BF16)


def _modulate(rows, x, layer):
    tm, d = rows.tm, rows.d
    return pl.pallas_call(
        _modulate_kernel, grid=(rows.n_rows // tm,),
        in_specs=[pl.BlockSpec((tm, d), lambda i: (i, 0)), rows.mod_spec(layer, 1), rows.mod_spec(layer, 0)],
        out_specs=pl.BlockSpec((tm, d), lambda i: (i, 0)),
        out_shape=jax.ShapeDtypeStruct((rows.n_rows, d), BF16),
        compiler_params=_params(1, 32), name="modulate",
    )(x, rows.mods, rows.mods)


def _mm_kernel(*refs, ksizes):
    b_ref, o_ref = refs[len(ksizes)], refs[-1]
    acc, off = None, 0
    for a_ref, k in zip(refs, ksizes):
        part = jnp.dot(a_ref[...], b_ref[off:off + k, :], preferred_element_type=F32)
        acc = part if acc is None else acc + part
        off += k
    o_ref[...] = acc.astype(o_ref.dtype)


def _mm(a_list, b, layer, tm, tn, out_dtype, vmem_mb, name):
    m = a_list[0].shape[0]
    ks = tuple(a.shape[1] for a in a_list)
    k_all, n = b.shape[1], b.shape[2]
    assert sum(ks) == k_all and m % tm == 0 and n % tn == 0
    in_specs = [pl.BlockSpec((tm, k), lambda i, j: (i, 0)) for k in ks]
    in_specs.append(pl.BlockSpec((None, k_all, tn), lambda i, j: (layer, 0, j)))
    return pl.pallas_call(
        functools.partial(_mm_kernel, ksizes=ks), grid=(m // tm, n // tn), in_specs=in_specs,
        out_specs=pl.BlockSpec((tm, tn), lambda i, j: (i, j)),
        out_shape=jax.ShapeDtypeStruct((m, n), out_dtype),
        compiler_params=_params(2, vmem_mb), name=name,
    )(*a_list, b)


def _rope_tables(pos, dh_a, dk_b):
    lane = jnp.arange(LANES)

    def tabs(dh):
        r, half = dh // 4, dh // 8
        inv = ROPE_THETA ** (-jnp.arange(half, dtype=F32) * (2.0 / r))
        ang = pos.astype(F32)[:, None] * inv[None, :]
        cos, sin = jnp.cos(ang), jnp.sin(ang)
        ld = lane % dh
        idx = ld % half
        c = jnp.where(ld < r, cos[:, idx], 1.0)
        s_up = jnp.where((ld >= half) & (ld < r), sin[:, idx], 0.0)
        s_dn = jnp.where(ld < half, -sin[:, idx], 0.0)
        return [c, s_up, s_dn]

    return jnp.concatenate(tabs(dh_a) + tabs(dk_b), axis=1).astype(F32)


def _post_kernel(*refs, w_a, w_b, w_c, half_a, half_b, n_prev, stacked):
    p_ref, tab_ref, lcg_ref, lcb_ref = refs[:4]
    qa_ref, ka_ref, va_ref, qb_ref, kb_ref, vb_ref, uc_ref, vc_ref = refs[4 + n_prev:12 + n_prev]
    cache_refs = refs[12 + n_prev:] if stacked else (None,) * 4
    tab = tab_ref[...]
    t = [tab[:, i * LANES:(i + 1) * LANES] for i in range(6)]

    def rope(x, c, s_up, s_dn, half):
        return x * c + pltpu.roll(x, half, 1) * s_up + pltpu.roll(x, LANES - half, 1) * s_dn

    def split_cols(src_off, dst_ref, width, tabs=None, half=None, slab_ref=None, head_major=False):
        for j in range(width // LANES):
            x = p_ref[:, src_off + j * LANES: src_off + (j + 1) * LANES]
            if tabs is not None:
                x = rope(x, *tabs, half)
            dst_ref[:, j * LANES:(j + 1) * LANES] = x.astype(dst_ref.dtype)
            if slab_ref is not None and head_major:
                slab_ref[j] = x
            elif slab_ref is not None:
                slab_ref[:, j, :] = x

    off = 0
    split_cols(off, qa_ref, w_a, t[0:3], half_a); off += w_a
    split_cols(off, ka_ref, w_a, t[0:3], half_a, cache_refs[0], True); off += w_a
    split_cols(off, va_ref, w_a, slab_ref=cache_refs[1], head_major=True); off += w_a
    split_cols(off, qb_ref, w_b, t[3:6], half_b); off += w_b
    split_cols(off, kb_ref, w_b, t[3:6], half_b, cache_refs[2]); off += w_b
    split_cols(off, vb_ref, w_b, slab_ref=cache_refs[3]); off += w_b
    uc_ref[...] = jax.nn.gelu(p_ref[:, off:off + w_c]); off += w_c
    g = jax.nn.gelu(p_ref[:, off:off + w_c])
    mu = jnp.mean(g, axis=-1, keepdims=True)
    gc = g - mu
    var = jnp.mean(gc * gc, axis=-1, keepdims=True)
    vc_ref[...] = gc * lax.rsqrt(var + LN_EPS) * lcg_ref[...] + lcb_ref[...]


def _post(rows, proj, ln_c_g, ln_c_b, layer, w_a, w_b, w_c, dh_a, dk_b, stacked=False, prev=()):
    tm, n = rows.tm, rows.n_rows
    depth = ln_c_g.shape[0]
    widths = (w_a, w_a, w_a, w_b, w_b, w_b, w_c, w_c)
    kv = BF16 if stacked else F32
    dtypes = (BF16, kv, kv, BF16, kv, kv, F32, F32)
    row_spec = lambda w: pl.BlockSpec((tm, w), lambda i: (i, 0))
    in_specs = [row_spec(proj.shape[1]), rows.rope_spec(),
                pl.BlockSpec((None, 1, w_c), lambda i: (layer, 0, 0)),
                pl.BlockSpec((None, 1, w_c), lambda i: (layer, 0, 0))]
    out_specs = [row_spec(w) for w in widths]
    out_shape = [jax.ShapeDtypeStruct((n, w), dt) for w, dt in zip(widths, dtypes)]
    aliases = {}
    if stacked:
        seq, per_seq = rows.seq, rows.seq // tm
        for w in (w_a, w_a):
            heads = w // LANES
            out_specs.append(pl.BlockSpec((None, None, heads, tm, LANES),
                                          lambda i: (layer, i // per_seq, 0, i % per_seq, 0)))
            out_shape.append(jax.ShapeDtypeStruct((depth, n // seq, heads, seq, LANES), F32))
        for w in (w_b, w_b):
            heads = w // LANES
            out_specs.append(pl.BlockSpec((None, tm, heads, LANES), lambda i: (layer, i, 0, 0)))
            out_shape.append(jax.ShapeDtypeStruct((depth, n, heads, LANES), F32))
        in_specs += [pl.BlockSpec(memory_space=pl.ANY)] * len(prev)
        aliases = {4 + k: 8 + k for k in range(len(prev))}
    return pl.pallas_call(
        functools.partial(_post_kernel, w_a=w_a, w_b=w_b, w_c=w_c, half_a=dh_a // 8, half_b=dk_b // 8,
                          n_prev=len(prev), stacked=stacked),
        grid=(n // tm,), in_specs=in_specs, out_specs=out_specs, out_shape=out_shape,
        input_output_aliases=aliases,
        compiler_params=_params(1, 48), name="proj_split",
    )(proj, rows.rope_tab, ln_c_g.reshape(depth, 1, w_c), ln_c_b.reshape(depth, 1, w_c), *prev)


def _heads_per_step(heads, most=4):
    return max(h for h in range(1, most + 1) if heads % h == 0)


def _moba_kernel(q_ref, k_ref, v_ref, o_ref, km_ref, *, nb, scale, hp):
    blk = MOBA_BLOCK
    qi = pl.program_id(2)

    @pl.when(qi == 0)
    def _():
        km_ref[...] = jnp.zeros_like(km_ref)
        for h in range(hp):
            for j in range(nb):
                kb = k_ref[j * blk:(j + 1) * blk, h * LANES:(h + 1) * LANES].astype(F32)
                km_ref[h, j:j + 1, :] = jnp.mean(kb, axis=0, keepdims=True)

    lane = lax.broadcasted_iota(jnp.int32, (blk, LANES), 1)
    row = lax.broadcasted_iota(jnp.int32, (blk, blk), 0)
    col = lax.broadcasted_iota(jnp.int32, (blk, blk), 1)

    def block(h, q, j):
        start = pl.multiple_of(j * blk, blk)
        kj = k_ref[pl.ds(start, blk), h * LANES:(h + 1) * LANES].astype(BF16)
        vj = v_ref[pl.ds(start, blk), h * LANES:(h + 1) * LANES].astype(BF16)
        return _nt(q, kj) * scale, vj

    qs, sels, states = [], [], []
    for h in range(hp):
        q = q_ref[:, h * LANES:(h + 1) * LANES]
        gate = _nt(q.astype(F32), km_ref[h], precision=HIGHEST)
        g = jnp.where(lane < qi, gate, NEG_INF)
        rank = jnp.zeros((blk, LANES), F32)
        for m in range(nb):
            gm = g[:, m:m + 1]
            tie_first = jnp.where(lane > m, 1.0, 0.0)
            rank = rank + jnp.where(gm > g, 1.0, jnp.where(gm == g, tie_first, 0.0))
        sels.append(jnp.where(lane < qi, jnp.where(rank < MOBA_TOPK, 1.0, 0.0), 0.0))
        s, vj = block(h, q, qi)
        s = jnp.where(col <= row, s, NEG_INF)
        m0 = jnp.max(s, axis=1, keepdims=True)
        p = jnp.exp(s - m0)
        states.append((m0, jnp.sum(p, axis=1, keepdims=True), jnp.dot(p.astype(BF16), vj, preferred_element_type=F32)))
        qs.append(q)

    def body(j, carry):
        out = []
        for h, (m_run, l_run, acc) in enumerate(carry):
            s, vj = block(h, qs[h], j)
            chosen = jnp.max(jnp.where(lane == j, sels[h], 0.0), axis=1, keepdims=True)
            s = jnp.where(chosen > 0.0, s, NEG_INF)
            m_new = jnp.maximum(m_run, jnp.max(s, axis=1, keepdims=True))
            alpha = jnp.exp(m_run - m_new)
            p = jnp.exp(s - m_new)
            out.append((m_new, alpha * l_run + jnp.sum(p, axis=1, keepdims=True),
                        alpha * acc + jnp.dot(p.astype(BF16), vj, preferred_element_type=F32)))
        return tuple(out)

    final = lax.fori_loop(0, qi, body, tuple(states))
    for h, (_, l_fin, acc) in enumerate(final):
        o_ref[:, h * LANES:(h + 1) * LANES] = (acc / l_fin).astype(o_ref.dtype)


def _moba_prompt(q, k, v, batch, seq, heads):
    nb = seq // MOBA_BLOCK
    hp = _heads_per_step(heads)
    assert seq % MOBA_BLOCK == 0 and nb <= LANES
    return pl.pallas_call(
        functools.partial(_moba_kernel, nb=nb, scale=LANES ** -0.5, hp=hp),
        grid=(batch, heads // hp, nb),
        in_specs=[pl.BlockSpec((MOBA_BLOCK, hp * LANES), lambda b, g, i: (b * nb + i, g)),
                  pl.BlockSpec((seq, hp * LANES), lambda b, g, i: (b, g)),
                  pl.BlockSpec((seq, hp * LANES), lambda b, g, i: (b, g))],
        out_specs=pl.BlockSpec((MOBA_BLOCK, hp * LANES), lambda b, g, i: (b * nb + i, g)),
        out_shape=jax.ShapeDtypeStruct(q.shape, BF16),
        scratch_shapes=[pltpu.VMEM((hp, LANES, LANES), F32)],
        compiler_params=_params(3, 32), name="moba_prompt",
    )(q, k, v)


def _lam(lamp_ref, lam_init):
    lp = lamp_ref[...]
    d1 = jnp.sum(lp[0:1] * lp[1:2], axis=1, keepdims=True)
    d2 = jnp.sum(lp[2:3] * lp[3:4], axis=1, keepdims=True)
    return jnp.exp(d1) - jnp.exp(d2) + lam_init


def _diff_finish(o, gain, lam_init):
    ms = jnp.mean(o * o, axis=-1, keepdims=True)
    return o * lax.rsqrt(ms + LN_EPS) * gain * (1.0 - lam_init)


def _diff_kernel(lamp_ref, gain_ref, q_ref, k_ref, v_ref, o_ref, *, tq, dk, lam_init, hp):
    qi = pl.program_id(2)
    scale = dk ** -0.5
    lam = _lam(lamp_ref, lam_init)
    lane = lax.broadcasted_iota(jnp.int32, (tq, LANES), 1)
    causal = lax.broadcasted_iota(jnp.int32, (tq, tq), 1) <= lax.broadcasted_iota(jnp.int32, (tq, tq), 0)

    def scores(h, qq, j):
        start = pl.multiple_of(j * tq, tq)
        kj = k_ref[pl.ds(start, tq), h * LANES:(h + 1) * LANES].astype(BF16)
        vj = v_ref[pl.ds(start, tq), h * LANES:(h + 1) * LANES].astype(BF16)
        return _nt(qq[0], kj) * scale, _nt(qq[1], kj) * scale, vj

    def first(s, vj):
        m = jnp.max(s, axis=1, keepdims=True)
        p = jnp.exp(s - m)
        return m, jnp.sum(p, axis=1, keepdims=True), jnp.dot(p.astype(BF16), vj, preferred_element_type=F32)

    def update(state, s, vj):
        m_run, l_run, acc = state
        m_new = jnp.maximum(m_run, jnp.max(s, axis=1, keepdims=True))
        alpha = jnp.exp(m_run - m_new)
        p = jnp.exp(s - m_new)
        return (m_new, alpha * l_run + jnp.sum(p, axis=1, keepdims=True),
                alpha * acc + jnp.dot(p.astype(BF16), vj, preferred_element_type=F32))

    qqs, states = [], []
    for h in range(hp):
        q = q_ref[:, h * LANES:(h + 1) * LANES]
        zero = jnp.zeros_like(q)
        qq = (jnp.where(lane < dk, q, zero), jnp.where(lane >= dk, q, zero))
        s1, s2, vj = scores(h, qq, qi)
        states.append((first(jnp.where(causal, s1, NEG_INF), vj), first(jnp.where(causal, s2, NEG_INF), vj)))
        qqs.append(qq)

    def body(j, carry):
        out = []
        for h, (a, b) in enumerate(carry):
            s1, s2, vj = scores(h, qqs[h], j)
            out.append((update(a, s1, vj), update(b, s2, vj)))
        return tuple(out)

    final = lax.fori_loop(0, qi, body, tuple(states))
    for h, (st1, st2) in enumerate(final):
        o = st1[2] / st1[1] - lam * (st2[2] / st2[1])
        o_ref[:, h * LANES:(h + 1) * LANES] = _diff_finish(o, gain_ref[...], lam_init).astype(o_ref.dtype)


def _diff_prompt(q, k, v, lamp, gain, layer, lam_init, batch, seq, heads, dk):
    tq = 256
    nq = seq // tq
    depth = lamp.shape[0]
    hp = _heads_per_step(heads)
    return pl.pallas_call(
        functools.partial(_diff_kernel, tq=tq, dk=dk, lam_init=lam_init, hp=hp),
        grid=(batch, heads // hp, nq),
        in_specs=[pl.BlockSpec((None, 4, dk), lambda b, g, i: (layer, 0, 0)),
                  pl.BlockSpec((None, 1, LANES), lambda b, g, i: (layer, 0, 0)),
                  pl.BlockSpec((tq, hp * LANES), lambda b, g, i: (b * nq + i, g)),
                  pl.BlockSpec((seq, hp * LANES), lambda b, g, i: (b, g)),
                  pl.BlockSpec((seq, hp * LANES), lambda b, g, i: (b, g))],
        out_specs=pl.BlockSpec((tq, hp * LANES), lambda b, g, i: (b * nq + i, g)),
        out_shape=jax.ShapeDtypeStruct(q.shape, BF16),
        compiler_params=_params(3, 32), name="diff_prompt",
    )(lamp, gain.reshape(depth, 1, LANES), q, k, v)


def _gmlp_kernel(u_ref, v_ref, w_ref, bt_ref, o_ref, *, groups, chunk):
    row = lax.broadcasted_iota(jnp.int32, (chunk, chunk), 0)
    col = lax.broadcasted_iota(jnp.int32, (chunk, chunk), 1)
    causal = col <= row
    for g in range(groups):
        sl = slice(g * LANES, (g + 1) * LANES)
        w = jnp.where(causal, w_ref[g], 0.0).astype(BF16)
        mixed = jnp.dot(w, v_ref[:, sl].astype(BF16), preferred_element_type=F32) + bt_ref[:, g:g + 1]
        o_ref[:, sl] = (u_ref[:, sl] * mixed).astype(o_ref.dtype)


def _gmlp_prompt(u, v, w_s, b_st, layer):
    n, w_c = u.shape
    groups, chunk = w_s.shape[1], w_s.shape[2]
    return pl.pallas_call(
        functools.partial(_gmlp_kernel, groups=groups, chunk=chunk), grid=(n // chunk,),
        in_specs=[pl.BlockSpec((chunk, w_c), lambda i: (i, 0)), pl.BlockSpec((chunk, w_c), lambda i: (i, 0)),
                  pl.BlockSpec((None, groups, chunk, chunk), lambda i: (layer, 0, 0, 0)),
                  pl.BlockSpec((None, chunk, groups), lambda i: (layer, 0, 0))],
        out_specs=pl.BlockSpec((chunk, w_c), lambda i: (i, 0)),
        out_shape=jax.ShapeDtypeStruct((n, w_c), BF16),
        compiler_params=_params(1, 32), name="gmlp_prompt",
    )(u, v, w_s, b_st)


def _gmlp_first_kernel(u_ref, v_ref, w_ref, bt_ref, o_ref, *, groups):
    for g in range(groups):
        sl = slice(g * LANES, (g + 1) * LANES)
        mixed = w_ref[g, 0:1, 0:1] * v_ref[:, sl] + bt_ref[0:1, g:g + 1]
        o_ref[:, sl] = (u_ref[:, sl] * mixed).astype(o_ref.dtype)


def _gmlp_sample(u, v, w_s, b_st, layer):
    n, w_c = u.shape
    groups, chunk = w_s.shape[1], w_s.shape[2]
    return pl.pallas_call(
        functools.partial(_gmlp_first_kernel, groups=groups), grid=(1,),
        in_specs=[pl.BlockSpec((n, w_c), lambda i: (0, 0)), pl.BlockSpec((n, w_c), lambda i: (0, 0)),
                  pl.BlockSpec((None, groups, chunk, chunk), lambda i: (layer, 0, 0, 0)),
                  pl.BlockSpec((None, chunk, groups), lambda i: (layer, 0, 0))],
        out_specs=pl.BlockSpec((n, w_c), lambda i: (0, 0)),
        out_shape=jax.ShapeDtypeStruct((n, w_c), BF16),
        compiler_params=_params(1, 32), name="gmlp_sample",
    )(u, v, w_s, b_st)


def _ln_kernel(*refs, alpha, emit_h, y_transposed, emit_ht):
    x_ref, y_ref, gate_ref, g_ref, b_ref = refs[:5]
    y = y_ref[...].T if y_transposed else y_ref[...]
    z = alpha * x_ref[...] + gate_ref[...] * y
    mu = jnp.mean(z, axis=-1, keepdims=True)
    zc = z - mu
    var = jnp.mean(zc * zc, axis=-1, keepdims=True)
    xn = zc * lax.rsqrt(var + LN_EPS) * g_ref[...] + b_ref[...]
    if emit_h:
        sc_ref, sh_ref, xo_ref, h_ref = refs[5:9]
        h = xn * (1.0 + sc_ref[...]) + sh_ref[...]
        h_ref[...] = h.astype(BF16)
        if emit_ht:
            refs[9][...] = h.T.astype(BF16)
    else:
        xo_ref = refs[5]
    xo_ref[...] = xn


def _ln(rows, x, y, ln_g, ln_b, layer, gate_k, alpha, next_mod=None, y_transposed=False, emit_ht=False):
    tm, d, n = min(rows.tm, 256), rows.d, rows.n_rows
    depth = ln_g.shape[0]
    row_spec = pl.BlockSpec((tm, d), lambda i: (i, 0))
    col_spec = pl.BlockSpec((d, tm), lambda i: (0, i))
    par_spec = pl.BlockSpec((None, 1, d), lambda i: (layer, 0, 0))
    in_specs = [row_spec, col_spec if y_transposed else row_spec, rows.mod_spec(layer, gate_k, tm), par_spec, par_spec]
    args = [x, y, rows.mods, ln_g.reshape(depth, 1, d), ln_b.reshape(depth, 1, d)]
    out_specs, out_shape = [row_spec], [jax.ShapeDtypeStruct((n, d), F32)]
    if next_mod is not None:
        nl, k_sc, k_sh = next_mod
        in_specs += [rows.mod_spec(nl, k_sc, tm), rows.mod_spec(nl, k_sh, tm)]
        args += [rows.mods, rows.mods]
        out_specs.append(row_spec)
        out_shape.append(jax.ShapeDtypeStruct((n, d), BF16))
        if emit_ht:
            out_specs.append(col_spec)
            out_shape.append(jax.ShapeDtypeStruct((d, n), BF16))
    out = pl.pallas_call(
        functools.partial(_ln_kernel, alpha=alpha, emit_h=next_mod is not None, y_transposed=y_transposed,
                          emit_ht=emit_ht),
        grid=(n // tm,), in_specs=in_specs, out_specs=out_specs, out_shape=out_shape,
        compiler_params=_params(1, 48), name="deepnorm_ln",
    )(*args)
    return tuple(out) if next_mod is not None else (out[0], None)


def _peer_select_kernel(q_ref, keys_ref, s1_ref, e1_ref, s2_ref, e2_ref, tau_ref, cand_ref, *, heads, n_keys, pairs):
    tm = q_ref.shape[0]
    sub = lax.broadcasted_iota(jnp.int32, (n_keys, tm), 0)
    csub = lax.broadcasted_iota(jnp.int32, cand_ref.shape, 0)

    def pop_max(x, iota, big):
        m = jnp.max(x, axis=0, keepdims=True)
        first = jnp.min(jnp.where(x == m, iota, big), axis=0, keepdims=True)
        return m, jnp.where(iota == first, NEG_INF, x)

    def top_rows(s):
        rows, x = [], s
        for _ in range(PEER_TOPK):
            m, x = pop_max(x, sub, n_keys)
            rows.append(m)
        return rows

    for h in range(heads):
        s1 = _nt(keys_ref[0], q_ref[:, (2 * h) * n_keys:(2 * h + 1) * n_keys], precision=HIGHEST)
        s2 = _nt(keys_ref[1], q_ref[:, (2 * h + 1) * n_keys:(2 * h + 2) * n_keys], precision=HIGHEST)
        a1, a2 = top_rows(s1), top_rows(s2)
        cand_ref[...] = jnp.full(cand_ref.shape, NEG_INF, F32)
        for r, (i, j) in enumerate(pairs):
            cand_ref[r:r + 1, :] = a1[i] + a2[j]
        c = cand_ref[...]
        x, tau = c, None
        for _ in range(PEER_TOPK):
            tau, x = pop_max(x, csub, cand_ref.shape[0])
        cmax = a1[0] + a2[0]
        z = jnp.sum(jnp.where(c >= tau, jnp.exp(c - cmax), 0.0), axis=0, keepdims=True)
        s1_ref[h] = s1
        s2_ref[h] = s2
        e1_ref[h] = jnp.exp(s1 - a1[0]) / z
        e2_ref[h] = jnp.exp(s2 - a2[0])
        tau_ref[h] = tau


def _peer_select(q, keys, layer, heads, tm):
    n, n_keys = q.shape[0], keys.shape[2]
    pairs = tuple((i, j) for i in range(PEER_TOPK) for j in range(PEER_TOPK) if (i + 1) * (j + 1) <= PEER_TOPK)
    n_cand = -(-len(pairs) // 8) * 8
    big = pl.BlockSpec((heads, n_keys, tm), lambda i: (0, 0, i))
    big_shape = jax.ShapeDtypeStruct((heads, n_keys, n), F32)
    return pl.pallas_call(
        functools.partial(_peer_select_kernel, heads=heads, n_keys=n_keys, pairs=pairs), grid=(n // tm,),
        in_specs=[pl.BlockSpec((tm, q.shape[1]), lambda i: (i, 0)),
                  pl.BlockSpec((None, 2, n_keys, keys.shape[3]), lambda i: (layer, 0, 0, 0))],
        out_specs=[big, big, big, big, pl.BlockSpec((heads, 1, tm), lambda i: (0, 0, i))],
        out_shape=[big_shape] * 4 + [jax.ShapeDtypeStruct((heads, 1, n), F32)],
        scratch_shapes=[pltpu.VMEM((n_cand, tm), F32)],
        compiler_params=_params(1, 32), name="peer_select",
    )(q, keys)


def _peer_dense_kernel(ht_ref, u_ref, vt_ref, s1_ref, e1_ref, s2_ref, e2_ref, tau_ref, o_ref, w_ref, g_ref,
                       *, heads, n_keys, n1_per_step):
    j = pl.program_id(1)
    tm = ht_ref.shape[1]

    @pl.when(j == 0)
    def _():
        o_ref[...] = jnp.zeros_like(o_ref)

    for r0 in range(0, n1_per_step, 2):
        pair = slice(r0 * n_keys, (r0 + 2) * n_keys)
        s1_rows = [[s1_ref[h, pl.ds(j * n1_per_step + r0 + k, 1), :] for k in range(2)] for h in range(heads)]
        e1_rows = [[e1_ref[h, pl.ds(j * n1_per_step + r0 + k, 1), :] for k in range(2)] for h in range(heads)]
        for c in range(tm // LANES):
            cols = slice(c * LANES, (c + 1) * LANES)
            g = [None, None]
            for h in range(heads):
                s2, e2, tau = s2_ref[h, :, cols], e2_ref[h, :, cols], tau_ref[h, :, cols]
                for k in range(2):
                    term = e1_rows[h][k][:, cols] * jnp.where(s1_rows[h][k][:, cols] + s2 >= tau, e2, 0.0)
                    g[k] = term if g[k] is None else g[k] + term
            for k in range(2):
                g_ref[(r0 + k) * n_keys:(r0 + k + 1) * n_keys, cols] = g[k]
        st = jnp.dot(u_ref[pair, :], ht_ref[...], preferred_element_type=F32)
        w_ref[pair, :] = (g_ref[pair, :] * jax.nn.gelu(st)).astype(BF16)
    rows_per_chunk = 512
    for m in range(vt_ref.shape[0] // rows_per_chunk):
        rows = slice(m * rows_per_chunk, (m + 1) * rows_per_chunk)
        o_ref[rows, :] += jnp.dot(vt_ref[rows, :], w_ref[...], preferred_element_type=F32)


def _peer_dense(h2t, u, vt, sel, layer, heads, tm, n1_per_step, vmem_mb):
    d, n = h2t.shape
    n_keys = sel[0].shape[1]
    et = n1_per_step * n_keys
    n_steps = n_keys * n_keys // et
    assert n1_per_step % 2 == 0 and n_keys % n1_per_step == 0
    once = pl.Buffered(1)
    resident = pl.BlockSpec((heads, n_keys, tm), lambda i, j: (0, 0, i), pipeline_mode=once)
    return pl.pallas_call(
        functools.partial(_peer_dense_kernel, heads=heads, n_keys=n_keys, n1_per_step=n1_per_step),
        grid=(n // tm, n_steps),
        in_specs=[pl.BlockSpec((d, tm), lambda i, j: (0, i), pipeline_mode=once),
                  pl.BlockSpec((None, et, d), lambda i, j: (layer, j, 0)),
                  pl.BlockSpec((None, d, et), lambda i, j: (layer, 0, j)),
                  resident, resident, resident, resident,
                  pl.BlockSpec((heads, 1, tm), lambda i, j: (0, 0, i))],
        out_specs=pl.BlockSpec((d, tm), lambda i, j: (0, i)),
        out_shape=jax.ShapeDtypeStruct((d, n), F32),
        scratch_shapes=[pltpu.VMEM((et, tm), BF16), pltpu.VMEM((et, tm), F32)],
        compiler_params=_params(2, vmem_mb), name="peer_dense",
    )(h2t, u, vt, *sel)


def _moba_decode_kernel(pt_ref, q_ref, kn_ref, vn_ref, *refs, pages_per_block, blocks_per_step, n_steps, scale):
    del pt_ref
    n_pg = pages_per_block * blocks_per_step
    k_refs, v_refs = refs[:n_pg], refs[n_pg:2 * n_pg]
    o_ref, ks_ref, m_ref, l_ref, acc_ref = refs[2 * n_pg:]
    step = pl.program_id(1)
    heads, page = k_refs[0].shape[0], k_refs[0].shape[1]
    qf = q_ref[...].astype(F32)

    for bb in range(blocks_per_step):
        pages = slice(bb * pages_per_block, (bb + 1) * pages_per_block)
        blk = step * blocks_per_step + bb
        s_pages, ksum = [], None
        for k_ref in k_refs[pages]:
            s_rows, ksum_rows = [], []
            for h in range(heads):
                kh = k_ref[h]
                q16 = jnp.broadcast_to(qf[h:h + 1, :], (16, LANES)).astype(BF16)
                s_rows.append(_nt(q16, kh.astype(BF16))[0:1, :])
                ksum_rows.append(jnp.sum(kh, axis=0, keepdims=True))
            s_pages.append(jnp.concatenate(s_rows, axis=0) * scale)
            part = jnp.concatenate(ksum_rows, axis=0)
            ksum = part if ksum is None else ksum + part
        m_blk = functools.reduce(jnp.maximum, [jnp.max(s, axis=1, keepdims=True) for s in s_pages])
        m_blk = jnp.broadcast_to(m_blk, qf.shape)
        l_blk, acc = None, None
        for s, v_ref in zip(s_pages, v_refs[pages]):
            pe = jnp.exp(s - m_blk)
            pv = jnp.concatenate(
                [jnp.dot(jnp.broadcast_to(pe[h:h + 1, :], (16, page)).astype(BF16), v_ref[h].astype(BF16),
                         preferred_element_type=F32)[0:1, :] for h in range(heads)], axis=0)
            l_part = jnp.sum(pe, axis=1, keepdims=True)
            l_blk = l_part if l_blk is None else l_blk + l_part
            acc = pv if acc is None else acc + pv
        m_ref[blk] = m_blk
        l_ref[blk] = jnp.broadcast_to(l_blk, qf.shape)
        acc_ref[blk] = acc
        ks_ref[blk] = ksum

    @pl.when(step == n_steps - 1)
    def _():
        shape = ks_ref.shape
        nb = shape[0]
        gate = jnp.sum(ks_ref[...] * (1.0 / MOBA_BLOCK) * qf, axis=-1, keepdims=True)
        g = jnp.broadcast_to(gate, shape)
        bi = lax.broadcasted_iota(jnp.int32, shape, 0)
        chosen = jnp.zeros(shape, F32)
        for _ in range(MOBA_TOPK):
            gmax = jnp.max(g, axis=0)
            first = jnp.min(jnp.where(g == gmax, bi, nb), axis=0)
            hit = bi == first
            chosen = jnp.where(hit, 1.0, chosen)
            g = jnp.where(hit, NEG_INF, g)
        s_own = jnp.broadcast_to(jnp.sum(qf * kn_ref[...], axis=-1, keepdims=True) * scale, qf.shape)
        m_all = m_ref[...]
        m_fin = jnp.maximum(s_own, jnp.max(jnp.where(chosen > 0.0, m_all, NEG_INF), axis=0))
        wts = jnp.where(chosen > 0.0, jnp.exp(m_all - m_fin), 0.0)
        w_own = jnp.exp(s_own - m_fin)
        l_fin = jnp.sum(wts * l_ref[...], axis=0) + w_own
        o = jnp.sum(wts * acc_ref[...], axis=0) + w_own * vn_ref[...]
        o_ref[...] = (o / l_fin).astype(o_ref.dtype)


def _moba_decode(page_table, q3, kn3, vn3, cache_k, cache_v, layer):
    batch, n_pages = page_table.shape
    heads, page = cache_k.shape[2], cache_k.shape[3]
    ppb = MOBA_BLOCK // page
    nb = n_pages // ppb
    assert n_pages % ppb == 0 and nb >= MOBA_TOPK and page == LANES
    bps = 2 if nb % 2 == 0 else 1
    n_pg = ppb * bps
    row = pl.BlockSpec((None, heads, LANES), lambda b, j, pt: (b, 0, 0))
    page_specs = [pl.BlockSpec((None, None, heads, page, LANES),
                               lambda b, j, pt, i=i: (layer, pt[b, j * n_pg + i], 0, 0, 0)) for i in range(n_pg)]
    return pl.pallas_call(
        functools.partial(_moba_decode_kernel, pages_per_block=ppb, blocks_per_step=bps, n_steps=nb // bps,
                          scale=LANES ** -0.5),
        grid_spec=pltpu.PrefetchScalarGridSpec(
            num_scalar_prefetch=1, grid=(batch, nb // bps),
            in_specs=[row, row, row] + page_specs + page_specs,
            out_specs=row,
            scratch_shapes=[pltpu.VMEM((nb, heads, LANES), F32)] * 4),
        out_shape=jax.ShapeDtypeStruct((batch, heads, LANES), BF16),
        compiler_params=_params(2, 32), name="moba_decode",
    )(page_table, q3, kn3, vn3, *([cache_k] * n_pg), *([cache_v] * n_pg))


def _diff_decode_kernel(pt_ref, lamp_ref, gain_ref, q_ref, kn_ref, vn_ref, *refs, dk, n_steps, pages_per_step, lam_init):
    del pt_ref
    k_refs, v_refs = refs[:pages_per_step], refs[pages_per_step:2 * pages_per_step]
    o_ref, m_ref, l_ref, acc_ref = refs[2 * pages_per_step:]
    p = pl.program_id(1)
    scale = dk ** -0.5
    qf = q_ref[...].astype(F32)
    heads = qf.shape[0]
    spread = jnp.where(lax.broadcasted_iota(jnp.int32, (LANES, 2 * LANES), 0) // dk
                       == lax.broadcasted_iota(jnp.int32, (LANES, 2 * LANES), 1) // LANES, 1.0, 0.0).astype(BF16)

    def map_scores(prod):
        hi = prod.astype(BF16)
        lo = (prod - hi.astype(F32)).astype(BF16)
        s = (jnp.dot(hi, spread, preferred_element_type=F32) + jnp.dot(lo, spread, preferred_element_type=F32)) * scale
        return s[:, :LANES], s[:, LANES:]

    @pl.when(p == 0)
    def _():
        s_new = map_scores(kn_ref[...] * qf)
        for i in range(2):
            m_ref[i] = s_new[i]
            l_ref[i] = jnp.ones_like(qf)
            acc_ref[i] = vn_ref[...]

    page = k_refs[0].shape[0]
    scores = [map_scores((k_ref[...] * qf).reshape(page * heads, LANES)) for k_ref in k_refs]
    for i in range(2):
        ss = [sc[i].reshape(page, heads, LANES) for sc in scores]
        m_old = m_ref[i]
        m_new = functools.reduce(jnp.maximum, [m_old] + [jnp.max(s, axis=0) for s in ss])
        alpha = jnp.exp(m_old - m_new)
        l_new, acc_new = alpha * l_ref[i], alpha * acc_ref[i]
        for s, v_ref in zip(ss, v_refs):
            pe = jnp.exp(s - m_new)
            l_new = l_new + jnp.sum(pe, axis=0)
            acc_new = acc_new + jnp.sum(pe * v_ref[...], axis=0)
        m_ref[i] = m_new
        l_ref[i] = l_new
        acc_ref[i] = acc_new

    @pl.when(p == n_steps - 1)
    def _():
        lam = _lam(lamp_ref, lam_init)
        o = acc_ref[0] / l_ref[0] - lam * (acc_ref[1] / l_ref[1])
        o_ref[...] = _diff_finish(o, gain_ref[...], lam_init).astype(o_ref.dtype)


def _diff_decode(page_table, q3, kn3, vn3, cache_k, cache_v, lamp, gain, layer, lam_init, dk):
    batch, n_pages = page_table.shape
    page, heads = cache_k.shape[2], cache_k.shape[3]
    depth = lamp.shape[0]
    pps = max(k for k in (4, 2, 1) if n_pages % k == 0)
    row = pl.BlockSpec((None, heads, LANES), lambda b, p, pt: (b, 0, 0))
    page_specs = [pl.BlockSpec((None, None, page, heads, LANES),
                               lambda b, p, pt, i=i: (layer, pt[b, p * pps + i], 0, 0, 0)) for i in range(pps)]
    return pl.pallas_call(
        functools.partial(_diff_decode_kernel, dk=dk, n_steps=n_pages // pps, pages_per_step=pps, lam_init=lam_init),
        grid_spec=pltpu.PrefetchScalarGridSpec(
            num_scalar_prefetch=1, grid=(batch, n_pages // pps),
            in_specs=[pl.BlockSpec((None, 4, dk), lambda b, p, pt: (layer, 0, 0)),
                      pl.BlockSpec((None, 1, LANES), lambda b, p, pt: (layer, 0, 0)),
                      row, row, row] + page_specs + page_specs,
            out_specs=row,
            scratch_shapes=[pltpu.VMEM((2, heads, LANES), F32)] * 3),
        out_shape=jax.ShapeDtypeStruct((batch, heads, LANES), BF16),
        compiler_params=_params(2, 32), name="diff_decode",
    )(page_table, lamp, gain.reshape(depth, 1, LANES), q3, kn3, vn3, *([cache_k] * pps), *([cache_v] * pps))


def kernel(x_prompt, x_sample, cache_a_k, cache_a_v, cache_b_k, cache_b_v, page_table, c_prompt, c_sample, ada_w, ada_b, w_in, w_o, lam_q1, lam_k1, lam_q2, lam_k2, diff_gain, ln_c_g, ln_c_b, w_s, b_s, ln1_g, ln1_b, ln2_g, ln2_b, peer_wq, peer_subkeys, peer_u, peer_v):
    batch, seq, d = x_prompt.shape
    dec_batch, dec_seq, _ = x_sample.shape
    depth = w_in.shape[0]
    n_phys, page, h_a, dh_a = cache_a_k.shape[1:]
    h_b, dv_b = cache_b_v.shape[3:]
    dk_b = cache_b_k.shape[4] // 2
    w_a, w_b, w_c = h_a * dh_a, h_b * dv_b, ln_c_g.shape[1]
    n_keys = peer_subkeys.shape[2]
    peer_heads = peer_wq.shape[2] // (2 * peer_subkeys.shape[3])
    past_len = page_table.shape[1] * page
    assert dec_seq == 1 and dh_a == LANES and dv_b == LANES and n_keys == LANES and peer_subkeys.shape[3] == LANES
    assert past_len % MOBA_BLOCK == 0 and w_s.shape[2] == LANES and dec_batch <= SAMPLE_ROWS
    alpha = (2 * depth) ** 0.25
    n_p, n_s = batch * seq, SAMPLE_ROWS

    w_in_b, w_o_b, wq_b = w_in.astype(BF16), w_o.astype(BF16), peer_wq.astype(BF16)
    u_b, vt_b = peer_u.astype(BF16), jnp.swapaxes(peer_v, 1, 2).astype(BF16)
    lamp = jnp.stack([lam_q1, lam_k1, lam_q2, lam_k2], axis=1)
    b_st = jnp.swapaxes(b_s, 1, 2)
    ck_a, cv_a = jnp.swapaxes(cache_a_k, 2, 3), jnp.swapaxes(cache_a_v, 2, 3)

    c_rows = -(-(batch + dec_batch) // 16) * 16
    c_all = jnp.zeros((c_rows, d), F32).at[:batch].set(c_prompt).at[batch:batch + dec_batch].set(c_sample)
    mods = _ada_mods(c_all, ada_w, ada_b)
    mods_p = mods[:, :batch].reshape(depth, batch, 6, 1, d)
    mods_s = mods[:, batch:batch + dec_batch].reshape(depth, dec_batch, 6, d).transpose(0, 2, 1, 3)
    mods_s = jnp.pad(mods_s, ((0, 0), (0, 0), (0, n_s - dec_batch), (0, 0)))

    rows_p = _Rows(n_p, 128, mods_p, _rope_tables(jnp.arange(seq), dh_a, dk_b), seq)
    rows_s = _Rows(n_s, n_s, mods_s, _rope_tables(jnp.full((n_s,), past_len), dh_a, dk_b), 1)

    x_p = x_prompt.reshape(n_p, d)
    x_s = jnp.pad(x_sample.reshape(dec_batch, d), ((0, n_s - dec_batch), (0, 0)))
    h_p = _modulate(rows_p, x_p, 0)
    h_s = _modulate(rows_s, x_s, 0)

    caches_p, outs_s = (), []
    for l in range(depth):
        lam_init = 0.8 - 0.6 * math.exp(-0.3 * l)
        nxt = (l + 1, 1, 0) if l + 1 < depth else None

        def mixer_inputs(rows, h, **cache_args):
            tm = _tile(rows.n_rows, 1024)
            proj = _mm([h], w_in_b, l, tm, _tile(w_in.shape[2], 768), F32, 48, "in_proj")
            return _post(rows, proj, ln_c_g, ln_c_b, l, w_a, w_b, w_c, dh_a, dk_b, **cache_args)

        def channel_mix(rows, x, mix_parts):
            tm = _tile(rows.n_rows, 1024)
            mix = _mm(mix_parts, w_o_b, l, tm, _tile(d, 512), F32, 48, "out_proj")
            x1, h2, h2_t = _ln(rows, x, mix, ln1_g, ln1_b, l, 2, alpha, (l, 4, 3), emit_ht=True)
            pq = _mm([h2], wq_b, l, tm, 512, F32, 48, "peer_query")
            sel = _peer_select(pq, peer_subkeys, l, peer_heads, _tile(rows.n_rows, 256))
            ffn_t = _peer_dense(h2_t, u_b, vt_b, sel, l, peer_heads, _tile(rows.n_rows, 512), 4, 56)
            return _ln(rows, x1, ffn_t, ln2_g, ln2_b, l, 5, alpha, nxt, y_transposed=True)

        qa, ka, va, qb, kb, vb, uc, vc, *caches_p = mixer_inputs(rows_p, h_p, stacked=True, prev=tuple(caches_p))
        oa = _moba_prompt(qa, ka, va, batch, seq, h_a)
        ob = _diff_prompt(qb, kb, vb, lamp, diff_gain, l, lam_init, batch, seq, h_b, dk_b)
        oc = _gmlp_prompt(uc, vc, w_s, b_st, l)
        x_p, h_p = channel_mix(rows_p, x_p, [oa, ob, oc])

        qa, ka, va, qb, kb, vb, uc, vc = mixer_inputs(rows_s, h_s)
        r3 = lambda a, heads: a[:dec_batch].reshape(dec_batch, heads, LANES)
        oa = _moba_decode(page_table, r3(qa, h_a), r3(ka, h_a), r3(va, h_a), ck_a, cv_a, l)
        ob = _diff_decode(page_table, r3(qb, h_b), r3(kb, h_b), r3(vb, h_b), cache_b_k, cache_b_v, lamp, diff_gain,
                          l, lam_init, dk_b)
        oc = _gmlp_sample(uc, vc, w_s, b_st, l)
        pad_rows = lambda a: jnp.pad(a.reshape(dec_batch, -1), ((0, n_s - dec_batch), (0, 0)))
        x_s, h_s = channel_mix(rows_s, x_s, [pad_rows(oa), pad_rows(ob), oc])
        outs_s.append((ka[:dec_batch], va[:dec_batch], kb[:dec_batch], vb[:dec_batch], vc[:dec_batch]))

    stack = lambda rows, i: jnp.stack([r[i] for r in rows], axis=0)
    return (x_p.reshape(batch, seq, d),
            x_s[:dec_batch].reshape(dec_batch, 1, d),
            jnp.swapaxes(caches_p[0], 2, 3),
            jnp.swapaxes(caches_p[1], 2, 3),
            caches_p[2].reshape(depth, batch, seq, h_b, 2 * dk_b),
            caches_p[3].reshape(depth, batch, seq, h_b, dv_b),
            stack(outs_s, 0).reshape(depth, dec_batch, 1, h_a, dh_a),
            stack(outs_s, 1).reshape(depth, dec_batch, 1, h_a, dh_a),
            stack(outs_s, 2).reshape(depth, dec_batch, 1, h_b, 2 * dk_b),
            stack(outs_s, 3).reshape(depth, dec_batch, 1, h_b, dv_b),
            stack(outs_s, 4).reshape(depth, dec_batch, 1, w_c))
```
